```python
import math
import jax, jax.numpy as jnp
from jax import lax
import numpy as np

D_MODEL = 1024
BATCH = 2
SEQ = 8192
DEPTH = 4
DEC_BATCH = 32
DEC_SEQ = 8
PAST_LEN = 8192
PAGE_SIZE = 128

N_META = 16
N_MIXERS = 3
H_SB = 16
DH_SB = D_MODEL // H_SB
H_DIFF = 8
DH_DIFF = D_MODEL // H_DIFF // 2
CONV_WIDTH = 3
D_FF = 7 * D_MODEL // 2
N_EXPERTS = 8
TOP_K = 2
Q_BLOCK = 128
EPS = 1e-6
N_SB = len(range(0, DEPTH, N_MIXERS))
N_CONV = len(range(1, DEPTH, N_MIXERS))
N_DIFF = len(range(2, DEPTH, N_MIXERS))
N_DENSE = len(range(0, DEPTH, 2))
N_MOE = len(range(1, DEPTH, 2))

kernel_name = 'hybrid_sb_conv_diff_moe_step'


def rms_norm(x, g):
    xf = x.astype(jnp.float32)
    y = xf * lax.rsqrt(jnp.mean(xf * xf, axis=-1, keepdims=True) + EPS)
    return (y * g.astype(jnp.float32)).astype(x.dtype)


def heads(x, n, d):
    return x.reshape(*x.shape[:-1], n, d)


def swiglu(h, w_gate, w_up, w_down):
    return (jax.nn.silu(h @ w_gate) * (h @ w_up)) @ w_down


def moe_swiglu(h, w_router, w_gate, w_up, w_down):
    logits = (h @ w_router).astype(jnp.float32)
    top_val, top_idx = lax.top_k(logits, TOP_K)
    gates = jax.nn.softmax(top_val, axis=-1)
    dense_gate = jnp.sum(jax.nn.one_hot(top_idx, N_EXPERTS, dtype=jnp.float32) * gates[..., None], axis=-2)
    out = jnp.zeros_like(h)
    for e in range(N_EXPERTS):
        out = out + dense_gate[..., e:e + 1].astype(h.dtype) * swiglu(h, w_gate[e], w_up[e], w_down[e])
    return out


def stick_breaking_attend(q, k, v, q_pos, k_pos):
    z = jnp.einsum('bqhd,bkhd->bhqk', q, k).astype(jnp.float32) * (q.shape[-1] ** -0.5)
    valid = k_pos[None, :] < q_pos[:, None]
    log_beta = jax.nn.log_sigmoid(z)
    log_keep = jnp.where(valid, jax.nn.log_sigmoid(-z), 0.0)
    log_keep_after = lax.cumsum(log_keep, axis=3, reverse=True) - log_keep
    w = jnp.where(valid, jnp.exp(log_beta + log_keep_after), 0.0)
    return jnp.einsum('bhqk,bkhd->bqhd', w.astype(v.dtype), v)


def diff_attend(q, k, v, q_pos, k_pos, lam):
    s = jnp.einsum('bqhd,bkhd->bhqk', q, k).astype(jnp.float32) * (q.shape[-1] ** -0.5)
    causal = k_pos[None, :] <= q_pos[:, None]
    p = jax.nn.softmax(jnp.where(causal, s, -jnp.inf), axis=-1)
    b, _, tq, tk = p.shape
    p = p.reshape(b, H_DIFF, 2, tq, tk)
    a = p[:, :, 0] - lam * p[:, :, 1]
    return jnp.einsum('bhqk,bkhd->bqhd', a.astype(v.dtype), v)


def sweep_query_blocks(attend, q, k, v):
    b, lp = q.shape[:2]
    pos = jnp.arange(lp, dtype=jnp.int32)
    out_meta = attend(q[:, :N_META], k, v, pos[:N_META], pos)
    n_blk = (lp - N_META) // Q_BLOCK
    q_blk = q[:, N_META:].reshape(b, n_blk, Q_BLOCK, *q.shape[2:]).swapaxes(0, 1)
    p_blk = pos[N_META:].reshape(n_blk, Q_BLOCK)
    out_blk = lax.map(lambda qp: attend(qp[0], k, v, qp[1], pos), (q_blk, p_blk))
    out_blk = out_blk.swapaxes(0, 1).reshape(b, lp - N_META, *out_blk.shape[3:])
    return jnp.concatenate([out_meta, out_blk], axis=1)


def causal_dwconv(u_ext, w):
    return lax.conv_general_dilated(
        u_ext, w[:, None, :].astype(u_ext.dtype), window_strides=(1,), padding='VALID',
        dimension_numbers=('NWC', 'WIO', 'NWC'), feature_group_count=u_ext.shape[-1])


def setup_inputs(seed: int = 0) -> dict:
    key = jax.random.key(seed)
    ks = jax.random.split(key, 32)
    f32 = jnp.float32
    n_pages = PAST_LEN // PAGE_SIZE
    n_used = DEC_BATCH * n_pages
    n_pool = n_used + n_used // 4

    def nrm(k, shape, scale=1.0):
        return jax.random.normal(k, shape, f32) * scale

    page_table = jax.random.permutation(ks[0], n_pool)[:n_used].reshape(DEC_BATCH, n_pages).astype(jnp.int32)
    return {
        'x_prompt': nrm(ks[1], (BATCH, SEQ, D_MODEL)),
        'x_sample': nrm(ks[2], (DEC_BATCH, DEC_SEQ, D_MODEL)),
        'cache_k_sb': nrm(ks[3], (N_SB, n_pool, PAGE_SIZE, H_SB, DH_SB)),
        'cache_v_sb': nrm(ks[4], (N_SB, n_pool, PAGE_SIZE, H_SB, DH_SB)),
        'cache_k_diff': nrm(ks[5], (N_DIFF, n_pool, PAGE_SIZE, 2 * H_DIFF, DH_DIFF)),
        'cache_v_diff': nrm(ks[6], (N_DIFF, n_pool, PAGE_SIZE, H_DIFF, 2 * DH_DIFF)),
        'state_conv': nrm(ks[7], (N_CONV, DEC_BATCH, CONV_WIDTH - 1, D_MODEL)),
        'page_table': page_table,
        'meta_tokens': nrm(ks[8], (N_META, D_MODEL)),
        'norm_mix': 1.0 + nrm(ks[9], (DEPTH, D_MODEL), 0.02),
        'w_mix_in': nrm(ks[10], (DEPTH, D_MODEL, 3 * D_MODEL), D_MODEL ** -0.5),
        'w_mix_out': nrm(ks[11], (DEPTH, D_MODEL, D_MODEL), D_MODEL ** -0.5),
        'conv_w': nrm(ks[12], (N_CONV, CONV_WIDTH, D_MODEL), CONV_WIDTH ** -0.5),
        'diff_lambda_q1': nrm(ks[13], (N_DIFF, DH_DIFF), 0.1),
        'diff_lambda_k1': nrm(ks[14], (N_DIFF, DH_DIFF), 0.1),
        'diff_lambda_q2': nrm(ks[15], (N_DIFF, DH_DIFF), 0.1),
        'diff_lambda_k2': nrm(ks[16], (N_DIFF, DH_DIFF), 0.1),
        'diff_subln': 1.0 + nrm(ks[17], (N_DIFF, 2 * DH_DIFF), 0.02),
        'norm_ffn': 1.0 + nrm(ks[18], (DEPTH, D_MODEL), 0.02),
        'w_dense_gate': nrm(ks[19], (N_DENSE, D_MODEL, D_FF), D_MODEL ** -0.5),
        'w_dense_up': nrm(ks[20], (N_DENSE, D_MODEL, D_FF), D_MODEL ** -0.5),
        'w_dense_down': nrm(ks[21], (N_DENSE, D_FF, D_MODEL), D_FF ** -0.5),
        'w_router': nrm(ks[22], (N_MOE, D_MODEL, N_EXPERTS), D_MODEL ** -0.5),
        'w_exp_gate': nrm(ks[23], (N_MOE, N_EXPERTS, D_MODEL, D_FF), D_MODEL ** -0.5),
        'w_exp_up': nrm(ks[24], (N_MOE, N_EXPERTS, D_MODEL, D_FF), D_MODEL ** -0.5),
        'w_exp_down': nrm(ks[25], (N_MOE, N_EXPERTS, D_FF, D_MODEL), D_FF ** -0.5),
        'norm_final': 1.0 + nrm(ks[26], (D_MODEL,), 0.02),
    }


def reference(x_prompt, x_sample, cache_k_sb, cache_v_sb, cache_k_diff, cache_v_diff, state_conv,
              page_table, meta_tokens, norm_mix, w_mix_in, w_mix_out, conv_w,
              diff_lambda_q1, diff_lambda_k1, diff_lambda_q2, diff_lambda_k2, diff_subln,
              norm_ffn, w_dense_gate, w_dense_up, w_dense_down,
              w_router, w_exp_gate, w_exp_up, w_exp_down, norm_final):
    bp = x_prompt.shape[0]
    bs, ts = x_sample.shape[:2]
    meta = jnp.broadcast_to(meta_tokens[None].astype(x_prompt.dtype), (bp, N_META, D_MODEL))
    xp = jnp.concatenate([meta, x_prompt], axis=1)
    xs = x_sample
    lp = xp.shape[1]
    past = page_table.shape[1] * cache_k_sb.shape[2]
    q_pos_s = past + jnp.arange(ts, dtype=jnp.int32)
    k_pos_s = jnp.arange(past + ts, dtype=jnp.int32)

    def gather_past(cache, j):
        g = cache[j, page_table]
        return g.reshape(g.shape[0], past, *g.shape[3:])

    sb_kp, sb_vp, sb_ks, sb_vs = [], [], [], []
    df_kp, df_vp, df_ks, df_vs = [], [], [], []
    cv_p, cv_s = [], []

    for i in range(DEPTH):
        kind, j = i % N_MIXERS, i // N_MIXERS
        up = rms_norm(xp, norm_mix[i]) @ w_mix_in[i]
        us = rms_norm(xs, norm_mix[i]) @ w_mix_in[i]
        if kind == 0:
            qp, kp, vp = [heads(t, H_SB, DH_SB) for t in jnp.split(up, 3, axis=-1)]
            qs, ks_, vs = [heads(t, H_SB, DH_SB) for t in jnp.split(us, 3, axis=-1)]
            op = sweep_query_blocks(stick_breaking_attend, qp, kp, vp)
            k_all = jnp.concatenate([gather_past(cache_k_sb, j).astype(ks_.dtype), ks_], axis=1)
            v_all = jnp.concatenate([gather_past(cache_v_sb, j).astype(vs.dtype), vs], axis=1)
            os_ = stick_breaking_attend(qs, k_all, v_all, q_pos_s, k_pos_s)
            sb_kp.append(kp); sb_vp.append(vp); sb_ks.append(ks_); sb_vs.append(vs)
            op = op.reshape(bp, lp, D_MODEL)
            os_ = os_.reshape(bs, ts, D_MODEL)
        elif kind == 1:
            gb_p, gc_p, hv_p = jnp.split(up, 3, axis=-1)
            gb_s, gc_s, hv_s = jnp.split(us, 3, axis=-1)
            u_ext_p = jnp.concatenate([jnp.zeros((bp, CONV_WIDTH - 1, D_MODEL), up.dtype), gc_p * hv_p], axis=1)
            u_ext_s = jnp.concatenate([state_conv[j].astype(us.dtype), gc_s * hv_s], axis=1)
            op = gb_p * causal_dwconv(u_ext_p, conv_w[j])
            os_ = gb_s * causal_dwconv(u_ext_s, conv_w[j])
            cv_p.append(u_ext_p[:, -(CONV_WIDTH - 1):])
            cv_s.append(u_ext_s[:, -(CONV_WIDTH - 1):])
        else:
            lam_init = 0.8 - 0.6 * math.exp(-0.3 * i)
            lam = (jnp.exp(jnp.sum(diff_lambda_q1[j].astype(jnp.float32) * diff_lambda_k1[j].astype(jnp.float32)))
                   - jnp.exp(jnp.sum(diff_lambda_q2[j].astype(jnp.float32) * diff_lambda_k2[j].astype(jnp.float32)))
                   + lam_init)
            qp_, kp_, vp_ = jnp.split(up, 3, axis=-1)
            qs_, ks2, vs_ = jnp.split(us, 3, axis=-1)
            qp_, kp_ = heads(qp_, 2 * H_DIFF, DH_DIFF), heads(kp_, 2 * H_DIFF, DH_DIFF)
            vp_ = heads(vp_, H_DIFF, 2 * DH_DIFF)
            qs_, ks2 = heads(qs_, 2 * H_DIFF, DH_DIFF), heads(ks2, 2 * H_DIFF, DH_DIFF)
            vs_ = heads(vs_, H_DIFF, 2 * DH_DIFF)
            attend = lambda q, k, v, qpos, kpos: diff_attend(q, k, v, qpos, kpos, lam)
            op = sweep_query_blocks(attend, qp_, kp_, vp_)
            k_all = jnp.concatenate([gather_past(cache_k_diff, j).astype(ks2.dtype), ks2], axis=1)
            v_all = jnp.concatenate([gather_past(cache_v_diff, j).astype(vs_.dtype), vs_], axis=1)
            os_ = attend(qs_, k_all, v_all, q_pos_s, k_pos_s)
            op = (rms_norm(op, diff_subln[j]) * (1.0 - lam_init)).reshape(bp, lp, D_MODEL)
            os_ = (rms_norm(os_, diff_subln[j]) * (1.0 - lam_init)).reshape(bs, ts, D_MODEL)
            df_kp.append(kp_); df_vp.append(vp_); df_ks.append(ks2); df_vs.append(vs_)
        xp = xp + op @ w_mix_out[i]
        xs = xs + os_ @ w_mix_out[i]

        hp = rms_norm(xp, norm_ffn[i])
        hs = rms_norm(xs, norm_ffn[i])
        m = i // 2
        if i % 2 == 0:
            xp = xp + swiglu(hp, w_dense_gate[m], w_dense_up[m], w_dense_down[m])
            xs = xs + swiglu(hs, w_dense_gate[m], w_dense_up[m], w_dense_down[m])
        else:
            xp = xp + moe_swiglu(hp, w_router[m], w_exp_gate[m], w_exp_up[m], w_exp_down[m])
            xs = xs + moe_swiglu(hs, w_router[m], w_exp_gate[m], w_exp_up[m], w_exp_down[m])

    y_prompt = rms_norm(xp, norm_final)[:, N_META:]
    y_sample = rms_norm(xs, norm_final)
    return (y_prompt, y_sample,
            jnp.stack(sb_kp), jnp.stack(sb_vp), jnp.stack(sb_ks), jnp.stack(sb_vs),
            jnp.stack(df_kp), jnp.stack(df_vp), jnp.stack(df_ks), jnp.stack(df_vs),
            jnp.stack(cv_p), jnp.stack(cv_s))
```

```python
import functools
import math

import jax
import jax.numpy as jnp
from jax import lax
from jax.experimental import pallas as pl
from jax.experimental.pallas import tpu as pltpu

F32 = jnp.float32
BF16 = jnp.bfloat16

D_MODEL = 1024
N_META = 16
N_MIXERS = 3
H_SB = 16
DH = 64
H_DIFF = 8
N_EXPERTS = 8
EPS = 1e-6
LANES = 128
SUBLANES = 8
VMEM_LIMIT = 56 * 1024 * 1024

ATT_BLK = 256
ROW_TILE = 512
FFN_TILE = 896
FFN_CHUNK = 512
MOE_CHUNK = 896
MOE_ROWS = 256
PAGES_PER_STEP = 4
NEG = -1e30


def _cp(sem, vmem=VMEM_LIMIT):
    return pltpu.CompilerParams(dimension_semantics=sem, vmem_limit_bytes=vmem)


def _rms(x, g):
    ms = jnp.mean(x * x, axis=-1, keepdims=True)
    return x * lax.rsqrt(ms + EPS) * g


def _dot(a, b):
    return jnp.dot(a, b, preferred_element_type=F32)


def _dot_nt(a, b):
    return lax.dot_general(a, b, (((1,), (1,)), ((), ())), preferred_element_type=F32)


def _split_bf16(x):
    hi = x.astype(BF16)
    lo = (x - hi.astype(F32)).astype(BF16)
    return hi, lo


def _silu(g):
    return g / (1.0 + jnp.exp(-g))


def _norm_mm_kernel(x_ref, g_ref, w_ref, *out_refs, plan):
    xn = _rms(x_ref[...], g_ref[...]).astype(BF16)
    oi = 0
    for c, (want_f32, bf_scale) in enumerate(plan):
        y = _dot(xn, w_ref[:, c * D_MODEL:(c + 1) * D_MODEL])
        if want_f32:
            out_refs[oi][...] = y
            oi += 1
        if bf_scale is not None:
            out_refs[oi][...] = (y * bf_scale).astype(BF16)
            oi += 1


def norm_matmul(x, g, w_bf, plan, tm=ROW_TILE):
    tt, d = x.shape
    n = w_bf.shape[1]
    out_shape, out_specs = [], []
    for want_f32, bf_scale in plan:
        for dt in ([F32] if want_f32 else []) + ([BF16] if bf_scale is not None else []):
            out_shape.append(jax.ShapeDtypeStruct((tt, d), dt))
            out_specs.append(pl.BlockSpec((tm, d), lambda i: (i, 0)))
    return pl.pallas_call(
        functools.partial(_norm_mm_kernel, plan=tuple(plan)),
        grid=(tt // tm,),
        in_specs=[pl.BlockSpec((tm, d), lambda i: (i, 0)),
                  pl.BlockSpec((1, d), lambda i: (0, 0)),
                  pl.BlockSpec((d, n), lambda i: (0, 0))],
        out_specs=out_specs,
        out_shape=out_shape,
        compiler_params=_cp(("parallel",)),
        name="norm_matmul",
    )(x, g.reshape(1, d), w_bf)


def _mm_res_kernel(o_ref, w_ref, x_ref, out_ref):
    out_ref[...] = x_ref[...] + _dot(o_ref[...].astype(BF16), w_ref[...])


def matmul_residual(o, w_bf, x, tm=ROW_TILE):
    tt, d = x.shape
    return pl.pallas_call(
        _mm_res_kernel,
        grid=(tt // tm,),
        in_specs=[pl.BlockSpec((tm, d), lambda i: (i, 0)),
                  pl.BlockSpec((d, d), lambda i: (0, 0)),
                  pl.BlockSpec((tm, d), lambda i: (i, 0))],
        out_specs=pl.BlockSpec((tm, d), lambda i: (i, 0)),
        out_shape=jax.ShapeDtypeStruct((tt, d), F32),
        compiler_params=_cp(("parallel",)),
        name="matmul_residual",
    )(o, w_bf, x)


def _ffn_kernel(x_ref, g_ref, wg_ref, wu_ref, wd_ref, out_ref, xn_ref, acc_ref):
    f = pl.program_id(1)

    @pl.when(f == 0)
    def _():
        x = x_ref[...]
        xn_ref[...] = _rms(x, g_ref[...]).astype(BF16)
        acc_ref[...] = x

    xn = xn_ref[...]
    a = (_silu(_dot(xn, wg_ref[...])) * _dot(xn, wu_ref[...])).astype(BF16)
    acc_ref[...] += _dot(a, wd_ref[...])

    @pl.when(f == pl.num_programs(1) - 1)
    def _():
        out_ref[...] = acc_ref[...]


def dense_ffn(x, g, wg, wu, wd, tm=FFN_TILE, tf=FFN_CHUNK):
    tt, d = x.shape
    dff = wg.shape[1]
    return pl.pallas_call(
        _ffn_kernel,
        grid=(tt // tm, dff // tf),
        in_specs=[pl.BlockSpec((tm, d), lambda i, f: (i, 0)),
                  pl.BlockSpec((1, d), lambda i, f: (0, 0)),
                  pl.BlockSpec((d, tf), lambda i, f: (0, f)),
                  pl.BlockSpec((d, tf), lambda i, f: (0, f)),
                  pl.BlockSpec((tf, d), lambda i, f: (f, 0))],
        out_specs=pl.BlockSpec((tm, d), lambda i, f: (i, 0)),
        out_shape=jax.ShapeDtypeStruct((tt, d), F32),
        scratch_shapes=[pltpu.VMEM((tm, d), BF16), pltpu.VMEM((tm, d), F32)],
        compiler_params=_cp(("parallel", "arbitrary")),
        name="dense_ffn",
    )(x, g.reshape(1, d), wg, wu, wd)


def _router_kernel(x_ref, g_ref, wr_ref, hn_ref, slot_ref, gate_ref, cnt_ref, *, tm):
    xn = _rms(x_ref[...], g_ref[...])
    hn_ref[...] = xn.astype(BF16)
    xh, xl = _split_bf16(xn)
    wh, wl = _split_bf16(wr_ref[...])
    logits = _dot(xh, wh) + _dot(xh, wl) + _dot(xl, wh)
    lane = lax.broadcasted_iota(jnp.int32, (tm, LANES), 1).astype(F32)
    logits = jnp.where(lane < N_EXPERTS, logits, NEG)
    m1 = jnp.max(logits, axis=1, keepdims=True)
    i1 = jnp.min(jnp.where(logits == m1, lane, float(LANES)), axis=1, keepdims=True)
    sel1 = lane == i1
    rest = jnp.where(sel1, NEG, logits)
    m2 = jnp.max(rest, axis=1, keepdims=True)
    i2 = jnp.min(jnp.where(rest == m2, lane, float(LANES)), axis=1, keepdims=True)
    sel2 = lane == i2
    e = jnp.exp(m2 - m1)
    gate_ref[...] = jnp.where(sel1, 1.0 / (1.0 + e), 0.0) + jnp.where(sel2, e / (1.0 + e), 0.0)
    sel = jnp.where(sel1 | sel2, 1.0, 0.0)
    r = lax.broadcasted_iota(jnp.int32, (tm, tm), 0)
    c = lax.broadcasted_iota(jnp.int32, (tm, tm), 1)
    before = jnp.where(r > c, 1.0, 0.0).astype(BF16)
    rank = _dot(before, sel.astype(BF16))
    slot_ref[...] = jnp.where(sel > 0.5, rank, -1.0)
    cnt_ref[...] = jnp.broadcast_to(jnp.sum(sel, axis=0, keepdims=True), (SUBLANES, LANES))


def router(x, g, w_router, tm):
    tt, d = x.shape
    nt = tt // tm
    wr = jnp.zeros((d, LANES), F32).at[:, :N_EXPERTS].set(w_router)
    return pl.pallas_call(
        functools.partial(_router_kernel, tm=tm),
        grid=(nt,),
        in_specs=[pl.BlockSpec((tm, d), lambda i: (i, 0)),
                  pl.BlockSpec((1, d), lambda i: (0, 0)),
                  pl.BlockSpec((d, LANES), lambda i: (0, 0))],
        out_specs=[pl.BlockSpec((tm, d), lambda i: (i, 0)),
                   pl.BlockSpec((tm, LANES), lambda i: (i, 0)),
                   pl.BlockSpec((tm, LANES), lambda i: (i, 0)),
                   pl.BlockSpec((None, SUBLANES, LANES), lambda i: (i, 0, 0))],
        out_shape=[jax.ShapeDtypeStruct((tt, d), BF16),
                   jax.ShapeDtypeStruct((tt, LANES), F32),
                   jax.ShapeDtypeStruct((tt, LANES), F32),
                   jax.ShapeDtypeStruct((nt, SUBLANES, LANES), F32)],
        compiler_params=_cp(("parallel",)),
        name="router",
    )(x, g.reshape(1, d), wr)


def _moe_kernel(cnt_ref, hn_ref, x_ref, slotc_ref, slotr_ref, gater_ref, wg_ref, wu_ref, wd_ref,
                out_ref, xc_ref, yacc_ref, *, tm, rows):
    i, e, f = pl.program_id(0), pl.program_id(1), pl.program_id(2)
    nf = pl.num_programs(2)
    nb = (cnt_ref[i * N_EXPERTS + e] + rows - 1) // rows
    slot_r = slotr_ref[...]

    def one_hot_rows(rb):
        rid = (lax.broadcasted_iota(jnp.int32, (rows, tm), 0) + rb * rows).astype(F32)
        return slot_r == rid

    @pl.when((e == 0) & (f == 0))
    def _():
        out_ref[...] = x_ref[...]

    @pl.when(f == 0)
    def _():
        def body(rb, carry):
            p = jnp.where(one_hot_rows(rb), 1.0, 0.0).astype(BF16)
            r0 = pl.multiple_of(rb * rows, rows)
            xc_ref[pl.ds(r0, rows), :] = _dot(p, hn_ref[...]).astype(BF16)
            return carry
        lax.fori_loop(0, nb, body, 0)

    def ffn_body(rb, carry):
        r0 = pl.multiple_of(rb * rows, rows)
        xs = xc_ref[pl.ds(r0, rows), :]
        a = (_silu(_dot(xs, wg_ref[...])) * _dot(xs, wu_ref[...])).astype(BF16)
        y = _dot(a, wd_ref[...])

        @pl.when(f == 0)
        def _():
            yacc_ref[pl.ds(r0, rows), :] = y

        @pl.when(f > 0)
        def _():
            yacc_ref[pl.ds(r0, rows), :] += y
        return carry
    lax.fori_loop(0, nb, ffn_body, 0)

    @pl.when(f == nf - 1)
    def _():
        lane = lax.broadcasted_iota(jnp.int32, (tm, LANES), 1)
        slot_c = jnp.sum(jnp.where(lane == e, slotc_ref[...], 0.0), axis=1, keepdims=True)
        gate_r = gater_ref[...]

        def body(rb, carry):
            r0 = pl.multiple_of(rb * rows, rows)
            gc = jnp.sum(jnp.where(one_hot_rows(rb), gate_r, 0.0), axis=1, keepdims=True)
            ys = (yacc_ref[pl.ds(r0, rows), :] * gc).astype(BF16)
            cid = (lax.broadcasted_iota(jnp.int32, (tm, rows), 1) + rb * rows).astype(F32)
            pt = jnp.where(slot_c == cid, 1.0, 0.0).astype(BF16)
            out_ref[...] += _dot(pt, ys)
            return carry
        lax.fori_loop(0, nb, body, 0)


def moe_ffn(x, g, w_router, wg, wu, wd, tm=FFN_TILE, tf=MOE_CHUNK, rows=MOE_ROWS):
    tt, d = x.shape
    nt = tt // tm
    dff = wg.shape[2]
    hn, slot, gate, cnt = router(x, g, w_router, tm)
    cnt_i = cnt[:, 0, :N_EXPERTS].astype(jnp.int32).reshape(nt * N_EXPERTS)

    def to_rows(a):
        a = a[:, :N_EXPERTS].reshape(nt, tm, N_EXPERTS)
        return a.transpose(0, 2, 1).reshape(nt, N_EXPERTS, 1, tm)

    grid_spec = pltpu.PrefetchScalarGridSpec(
        num_scalar_prefetch=1,
        grid=(nt, N_EXPERTS, dff // tf),
        in_specs=[pl.BlockSpec((tm, d), lambda i, e, f, c: (i, 0)),
                  pl.BlockSpec((tm, d), lambda i, e, f, c: (i, 0)),
                  pl.BlockSpec((tm, LANES), lambda i, e, f, c: (i, 0)),
                  pl.BlockSpec((None, None, 1, tm), lambda i, e, f, c: (i, e, 0, 0)),
                  pl.BlockSpec((None, None, 1, tm), lambda i, e, f, c: (i, e, 0, 0)),
                  pl.BlockSpec((None, d, tf), lambda i, e, f, c: (e, 0, f)),
                  pl.BlockSpec((None, d, tf), lambda i, e, f, c: (e, 0, f)),
                  pl.BlockSpec((None, tf, d), lambda i, e, f, c: (e, f, 0))],
        out_specs=pl.BlockSpec((tm, d), lambda i, e, f, c: (i, 0)),
        scratch_shapes=[pltpu.VMEM((_round_up(tm, rows), d), BF16), pltpu.VMEM((_round_up(tm, rows), d), F32)],
    )
    return pl.pallas_call(
        functools.partial(_moe_kernel, tm=tm, rows=rows),
        grid_spec=grid_spec,
        out_shape=jax.ShapeDtypeStruct((tt, d), F32),
        compiler_params=_cp(("parallel", "arbitrary", "arbitrary")),
        name="moe_ffn",
    )(cnt_i, hn, x, slot, to_rows(slot), to_rows(gate), wg, wu, wd)


def _after_matrix(n):
    r = lax.broadcasted_iota(jnp.int32, (n, n), 0)
    c = lax.broadcasted_iota(jnp.int32, (n, n), 1)
    return jnp.where(r > c, 1.0, 0.0).astype(BF16)


def _sb_block(z, valid, after, carry):
    t = jnp.log(1.0 + jnp.exp(-jnp.abs(z)))
    log_beta = jnp.minimum(z, 0.0) - t
    log_keep = jnp.minimum(-z, 0.0) - t
    if valid is not None:
        log_keep = jnp.where(valid, log_keep, 0.0)
    hi, lo = _split_bf16(log_keep)
    keep_after = _dot(hi, after) + _dot(lo, after)
    w = jnp.exp(log_beta + keep_after + carry)
    if valid is not None:
        w = jnp.where(valid, w, 0.0)
    return w, carry + keep_after[:, 0:1] + log_keep[:, 0:1]


def _head_pair(q2):
    lane = lax.broadcasted_iota(jnp.int32, q2.shape, 1)
    zero = jnp.zeros_like(q2)
    return jnp.where(lane < DH, q2, zero), jnp.where(lane >= DH, q2, zero)


def _sb_prompt_kernel(o_hbm_ref, q_ref, k_ref, v_ref, o_ref, acc_ref, car_ref, *, blk):
    del o_hbm_ref
    qi = pl.program_id(2)
    qh = _head_pair(q_ref[...])
    after = _after_matrix(blk)
    acc_ref[...] = jnp.zeros_like(acc_ref)
    car_ref[...] = jnp.zeros_like(car_ref)
    row = lax.broadcasted_iota(jnp.int32, (blk, blk), 0)
    col = lax.broadcasted_iota(jnp.int32, (blk, blk), 1)

    def block(kj, valid):
        k0 = pl.multiple_of(kj * blk, blk)
        k2 = k_ref[pl.ds(k0, blk), :]
        v2 = v_ref[pl.ds(k0, blk), :]
        for h in range(2):
            w, car = _sb_block(_dot_nt(qh[h], k2), valid, after, car_ref[h])
            car_ref[h] = car
            acc_ref[h] += _dot(w.astype(BF16), v2)

    block(qi, col < row)

    def body(s, carry):
        block(qi - 1 - s, None)
        return carry
    lax.fori_loop(0, qi, body, 0)
    lane = lax.broadcasted_iota(jnp.int32, (blk, LANES), 1)
    o_ref[...] = jnp.where(lane < DH, acc_ref[0], acc_ref[1])


def sb_prompt_attention(o_init, q_bf, k_bf, v_bf, n_batch, lpad, blk=ATT_BLK):
    tt, d = q_bf.shape
    nq = lpad // blk
    return pl.pallas_call(
        functools.partial(_sb_prompt_kernel, blk=blk),
        grid=(n_batch, d // LANES, nq),
        in_specs=[pl.BlockSpec(memory_space=pl.ANY),
                  pl.BlockSpec((blk, LANES), lambda b, h, i: (b * nq + i, h)),
                  pl.BlockSpec((lpad, LANES), lambda b, h, i: (b, h)),
                  pl.BlockSpec((lpad, LANES), lambda b, h, i: (b, h))],
        out_specs=pl.BlockSpec((blk, LANES), lambda b, h, i: (b * nq + i, h)),
        out_shape=jax.ShapeDtypeStruct((tt, d), F32),
        scratch_shapes=[pltpu.VMEM((2, blk, LANES), F32), pltpu.VMEM((2, blk, 1), F32)],
        input_output_aliases={0: 0},
        compiler_params=_cp(("parallel", "parallel", "arbitrary")),
        name="sb_prompt_attention",
    )(o_init, q_bf, k_bf, v_bf)


def _lambda(lam_ref, lam_init):
    a = jnp.sum(lam_ref[0:1, :] * lam_ref[1:2, :], axis=1, keepdims=True)
    b = jnp.sum(lam_ref[2:3, :] * lam_ref[3:4, :], axis=1, keepdims=True)
    return jnp.exp(a) - jnp.exp(b) + lam_init


def _softmax_block(s, v, m_old, l_old, acc_old):
    m_new = jnp.maximum(m_old, jnp.max(s, axis=1, keepdims=True))
    alpha = jnp.exp(m_old - m_new)
    p = jnp.exp(s - m_new)
    l_new = alpha * l_old + jnp.sum(p, axis=1, keepdims=True)
    return m_new, l_new, alpha * acc_old + _dot(p.astype(BF16), v)


def _diff_prompt_kernel(o_hbm_ref, lam_ref, g_ref, q_ref, k_ref, v_ref, o_ref, m_ref, l_ref, acc_ref,
                        *, blk, lam_init):
    del o_hbm_ref
    qi = pl.program_id(2)
    qh = _head_pair(q_ref[...])
    m_ref[...] = jnp.full_like(m_ref, NEG)
    l_ref[...] = jnp.zeros_like(l_ref)
    acc_ref[...] = jnp.zeros_like(acc_ref)
    row = lax.broadcasted_iota(jnp.int32, (blk, blk), 0)
    col = lax.broadcasted_iota(jnp.int32, (blk, blk), 1)

    def block(kj, valid):
        k0 = pl.multiple_of(kj * blk, blk)
        k2 = k_ref[pl.ds(k0, blk), :]
        v2 = v_ref[pl.ds(k0, blk), :]
        for h in range(2):
            s = _dot_nt(qh[h], k2)
            if valid is not None:
                s = jnp.where(valid, s, NEG)
            m_ref[h], l_ref[h], acc_ref[h] = _softmax_block(s, v2, m_ref[h], l_ref[h], acc_ref[h])

    block(qi, col <= row)

    def body(s, carry):
        block(s, None)
        return carry
    lax.fori_loop(0, qi, body, 0)
    lam = _lambda(lam_ref, lam_init)
    o = acc_ref[0] / l_ref[0] - lam * (acc_ref[1] / l_ref[1])
    o_ref[...] = _rms(o, g_ref[...]) * (1.0 - lam_init)


def diff_prompt_attention(o_init, lam4, subln, q_bf, k_bf, v_bf, n_batch, lpad, lam_init, blk=ATT_BLK):
    tt, d = q_bf.shape
    nq = lpad // blk
    return pl.pallas_call(
        functools.partial(_diff_prompt_kernel, blk=blk, lam_init=lam_init),
        grid=(n_batch, d // LANES, nq),
        in_specs=[pl.BlockSpec(memory_space=pl.ANY),
                  pl.BlockSpec((4, DH), lambda b, h, i: (0, 0)),
                  pl.BlockSpec((1, LANES), lambda b, h, i: (0, 0)),
                  pl.BlockSpec((blk, LANES), lambda b, h, i: (b * nq + i, h)),
                  pl.BlockSpec((lpad, LANES), lambda b, h, i: (b, h)),
                  pl.BlockSpec((lpad, LANES), lambda b, h, i: (b, h))],
        out_specs=pl.BlockSpec((blk, LANES), lambda b, h, i: (b * nq + i, h)),
        out_shape=jax.ShapeDtypeStruct((tt, d), F32),
        scratch_shapes=[pltpu.VMEM((2, blk, 1), F32), pltpu.VMEM((2, blk, 1), F32),
                        pltpu.VMEM((2, blk, LANES), F32)],
        input_output_aliases={0: 0},
        compiler_params=_cp(("parallel", "parallel", "arbitrary")),
        name="diff_prompt_attention",
    )(o_init, lam4, subln.reshape(1, LANES), q_bf, k_bf, v_bf)


def _block_diag_mask(n_heads_rows, lane_group):
    shape = (n_heads_rows * SUBLANES, D_MODEL)
    rh = lax.broadcasted_iota(jnp.int32, shape, 0) // SUBLANES
    lh = lax.broadcasted_iota(jnp.int32, shape, 1) // lane_group
    return rh, lh


def _sample_setup(q_ref, kn_ref, vn_ref, qbd_ref, kpad_ref, vpad_ref):
    q = q_ref[...] * (DH ** -0.5)
    qt = jnp.broadcast_to(q[None], (H_SB, SUBLANES, D_MODEL)).reshape(H_SB * SUBLANES, D_MODEL)
    rh, lh = _block_diag_mask(H_SB, DH)
    qbd_ref[...] = jnp.where(rh == lh, qt, 0.0).astype(BF16)
    kpad_ref[...] = jnp.zeros_like(kpad_ref)
    vpad_ref[...] = jnp.zeros_like(vpad_ref)
    kpad_ref[0:SUBLANES, :] = kn_ref[...]
    vpad_ref[0:SUBLANES, :] = vn_ref[...]


def _sb_sample_kernel(pt_ref, o_hbm_ref, q_ref, kn_ref, vn_ref, *rest, n_pg):
    del pt_ref, o_hbm_ref
    k_refs, v_refs = rest[:n_pg], rest[n_pg:2 * n_pg]
    o_ref, qbd_ref, acc_ref, car_ref, kpad_ref, vpad_ref = rest[2 * n_pg:]
    st = pl.program_id(1)
    after = _after_matrix(LANES)

    def chunk(kb, vb, valid):
        w, car = _sb_block(_dot_nt(qbd_ref[...], kb), valid, after, car_ref[...])
        car_ref[...] = car
        acc_ref[...] += _dot(w.astype(BF16), vb)

    @pl.when(st == 0)
    def _():
        _sample_setup(q_ref, kn_ref, vn_ref, qbd_ref, kpad_ref, vpad_ref)
        acc_ref[...] = jnp.zeros_like(acc_ref)
        car_ref[...] = jnp.zeros_like(car_ref)
        key = lax.broadcasted_iota(jnp.int32, (LANES, LANES), 1)
        tq = lax.broadcasted_iota(jnp.int32, (LANES, LANES), 0) % SUBLANES
        chunk(kpad_ref[...].astype(BF16), vpad_ref[...].astype(BF16), key < tq)

    @pl.when(st > 0)
    def _():
        for p in reversed(range(n_pg)):
            chunk(k_refs[p][...].astype(BF16), v_refs[p][...].astype(BF16), None)

    @pl.when(st == pl.num_programs(1) - 1)
    def _():
        rh, lh = _block_diag_mask(H_SB, DH)
        a = jnp.where(rh == lh, acc_ref[...], 0.0).reshape(H_SB, SUBLANES, D_MODEL)
        o_ref[...] = jnp.sum(a, axis=0)


def _diff_sample_kernel(pt_ref, o_hbm_ref, lam_ref, g_ref, q_ref, kn_ref, vn_ref, *rest, n_pg, lam_init):
    del pt_ref, o_hbm_ref
    k_refs, v_refs = rest[:n_pg], rest[n_pg:2 * n_pg]
    o_ref, qbd_ref, acc_ref, m_ref, l_ref, kpad_ref, vpad_ref = rest[2 * n_pg:]
    st = pl.program_id(1)

    def chunk(kb, vb, valid):
        s = _dot_nt(qbd_ref[...], kb)
        if valid is not None:
            s = jnp.where(valid, s, NEG)
        m_ref[...], l_ref[...], acc_ref[...] = _softmax_block(s, vb, m_ref[...], l_ref[...], acc_ref[...])

    @pl.when(st == 0)
    def _():
        _sample_setup(q_ref, kn_ref, vn_ref, qbd_ref, kpad_ref, vpad_ref)
        acc_ref[...] = jnp.zeros_like(acc_ref)
        m_ref[...] = jnp.full_like(m_ref, NEG)
        l_ref[...] = jnp.zeros_like(l_ref)
        key = lax.broadcasted_iota(jnp.int32, (LANES, LANES), 1)
        tq = lax.broadcasted_iota(jnp.int32, (LANES, LANES), 0) % SUBLANES
        chunk(kpad_ref[...].astype(BF16), vpad_ref[...].astype(BF16), key <= tq)

    @pl.when(st > 0)
    def _():
        for p in range(n_pg):
            chunk(k_refs[p][...].astype(BF16), v_refs[p][...].astype(BF16), None)

    @pl.when(st == pl.num_programs(1) - 1)
    def _():
        lam = _lambda(lam_ref, lam_init)
        rh, lh = _block_diag_mask(2 * H_DIFF, 2 * DH)
        coef = jnp.where(rh % 2 == 0, 1.0, -lam)
        a = jnp.where(rh // 2 == lh, acc_ref[...] / l_ref[...] * coef, 0.0)
        o = jnp.sum(a.reshape(2 * H_DIFF, SUBLANES, D_MODEL), axis=0)
        for h in range(H_DIFF):
            sl = slice(h * 2 * DH, (h + 1) * 2 * DH)
            o_ref[:, sl] = _rms(o[:, sl], g_ref[...]) * (1.0 - lam_init)


def sample_attention(kind, o_init, page_table, q, k_new, v_new, cache_k, cache_v, layer, row0, n_seq,
                     lam4=None, subln=None, lam_init=None, n_pg=PAGES_PER_STEP):
    tt, d = q.shape
    n_pages = page_table.shape[1]
    page = cache_k.shape[2]
    assert page == LANES and n_pages % n_pg == 0 and row0 % SUBLANES == 0
    n_grp = n_pages // n_pg
    n_pool = cache_k.shape[1]
    ck = cache_k.reshape(cache_k.shape[0], n_pool, page, d)
    cv = cache_v.reshape(cache_v.shape[0], n_pool, page, d)
    blk0 = row0 // SUBLANES

    def tok_spec():
        return pl.BlockSpec((SUBLANES, d), lambda s, st, pt: (blk0 + s, 0))

    def page_spec(p):
        if kind == "sb":
            def im(s, st, pt):
                grp = jnp.minimum(n_grp - st, n_grp - 1)
                return (layer, pt[s, grp * n_pg + p], 0, 0)
        else:
            def im(s, st, pt):
                grp = jnp.maximum(st - 1, 0)
                return (layer, pt[s, grp * n_pg + p], 0, 0)
        return pl.BlockSpec((None, None, page, d), im)

    rows = H_SB * SUBLANES
    common_scratch = [pltpu.VMEM((rows, d), BF16), pltpu.VMEM((rows, d), F32)]
    pad_scratch = [pltpu.VMEM((LANES, d), F32), pltpu.VMEM((LANES, d), F32)]
    page_specs = [page_spec(p) for p in range(n_pg)] * 2
    page_args = [ck] * n_pg + [cv] * n_pg
    if kind == "sb":
        body = functools.partial(_sb_sample_kernel, n_pg=n_pg)
        extra_specs, extra_args = [], []
        scratch = common_scratch + [pltpu.VMEM((rows, 1), F32)] + pad_scratch
    else:
        body = functools.partial(_diff_sample_kernel, n_pg=n_pg, lam_init=lam_init)
        extra_specs = [pl.BlockSpec((4, DH), lambda s, st, pt: (0, 0)),
                       pl.BlockSpec((1, LANES), lambda s, st, pt: (0, 0))]
        extra_args = [lam4, subln.reshape(1, LANES)]
        scratch = common_scratch + [pltpu.VMEM((rows, 1), F32), pltpu.VMEM((rows, 1), F32)] + pad_scratch
    grid_spec = pltpu.PrefetchScalarGridSpec(
        num_scalar_prefetch=1,
        grid=(n_seq, n_grp + 1),
        in_specs=[pl.BlockSpec(memory_space=pl.ANY)] + extra_specs
                 + [tok_spec(), tok_spec(), tok_spec()] + page_specs,
        out_specs=tok_spec(),
        scratch_shapes=scratch,
    )
    n_in = 1 + len(extra_args)
    return pl.pallas_call(
        body,
        grid_spec=grid_spec,
        out_shape=jax.ShapeDtypeStruct((tt, d), F32),
        input_output_aliases={1: 0},
        compiler_params=_cp(("parallel", "arbitrary")),
        name=kind + "_sample_attention",
    )(page_table, o_init, *extra_args, q, k_new, v_new, *page_args)


def _conv_kernel(gb_ref, gc_ref, hv_ref, hgc_ref, hhv_ref, p1_ref, p2_ref, w_ref, o_ref, u_ref,
                 *, tiles_per_seq, sample_tile, dec_seq):
    i = pl.program_id(0)
    u = gc_ref[...] * hv_ref[...]
    u_ref[...] = u
    row = lax.broadcasted_iota(jnp.int32, u.shape, 0)
    r1 = pltpu.roll(u, 1, axis=0)
    r2 = pltpu.roll(u, 2, axis=0)
    w0, w1, w2 = w_ref[0:1, :], w_ref[1:2, :], w_ref[2:3, :]

    def finish(prev1, prev2):
        o_ref[...] = gb_ref[...] * (w0 * prev2 + w1 * prev1 + w2 * u)

    @pl.when(i == sample_tile)
    def _():
        pos = row % dec_seq
        finish(jnp.where(pos == 0, p1_ref[...], r1), jnp.where(pos < 2, p2_ref[...], r2))

    @pl.when(i != sample_tile)
    def _():
        hu = hgc_ref[...] * hhv_ref[...]
        hu = jnp.where(i % tiles_per_seq == 0, 0.0, hu)
        h7, h6 = hu[7:8, :], hu[6:7, :]
        finish(jnp.where(row == 0, h7, r1), jnp.where(row == 0, h6, jnp.where(row == 1, h7, r2)))


def gated_conv(gb, gc, hv, prev1, prev2, w, lpad, sample_row0, dec_seq, tc=ATT_BLK):
    tt, d = gb.shape
    assert lpad % tc == 0 and sample_row0 % tc == 0 and prev1.shape[0] == tc
    per = tc // SUBLANES

    def tile():
        return pl.BlockSpec((tc, d), lambda i: (i, 0))

    def halo():
        return pl.BlockSpec((SUBLANES, d), lambda i: (jnp.maximum(i * per - 1, 0), 0))

    def whole(r):
        return pl.BlockSpec((r, d), lambda i: (0, 0))

    return pl.pallas_call(
        functools.partial(_conv_kernel, tiles_per_seq=lpad // tc, sample_tile=sample_row0 // tc,
                          dec_seq=dec_seq),
        grid=(tt // tc,),
        in_specs=[tile(), tile(), tile(), halo(), halo(), whole(tc), whole(tc), whole(w.shape[0])],
        out_specs=[tile(), tile()],
        out_shape=[jax.ShapeDtypeStruct((tt, d), F32), jax.ShapeDtypeStruct((tt, d), F32)],
        compiler_params=_cp(("parallel",)),
        name="gated_conv",
    )(gb, gc, hv, gc, hv, prev1, prev2, w)


def _norm_kernel(x_ref, g_ref, o_ref):
    o_ref[...] = _rms(x_ref[...], g_ref[...])


def final_norm(x, g, tm=ROW_TILE):
    tt, d = x.shape
    return pl.pallas_call(
        _norm_kernel,
        grid=(tt // tm,),
        in_specs=[pl.BlockSpec((tm, d), lambda i: (i, 0)), pl.BlockSpec((1, d), lambda i: (0, 0))],
        out_specs=pl.BlockSpec((tm, d), lambda i: (i, 0)),
        out_shape=jax.ShapeDtypeStruct((tt, d), F32),
        compiler_params=_cp(("parallel",)),
        name="final_norm",
    )(x, g.reshape(1, d))


def _round_up(a, b):
    return -(-a // b) * b


def kernel(x_prompt, x_sample, cache_k_sb, cache_v_sb, cache_k_diff, cache_v_diff, state_conv, page_table,
           meta_tokens, norm_mix, w_mix_in, w_mix_out, conv_w, diff_lambda_q1, diff_lambda_k1,
           diff_lambda_q2, diff_lambda_k2, diff_subln, norm_ffn, w_dense_gate, w_dense_up, w_dense_down,
           w_router, w_exp_gate, w_exp_up, w_exp_down, norm_final):
    bp, seq, d = x_prompt.shape
    bs, ts, _ = x_sample.shape
    depth = w_mix_in.shape[0]
    assert d == D_MODEL and ts == SUBLANES and H_SB * ts == LANES
    lp = seq + N_META
    lpad = _round_up(lp, ATT_BLK)
    n_s = bs * ts
    row_s = bp * lpad
    tt = _round_up(row_s + n_s, math.lcm(ROW_TILE, FFN_TILE, ATT_BLK))
    assert n_s == ATT_BLK

    meta = jnp.broadcast_to(meta_tokens[None].astype(F32), (bp, N_META, d))
    xp = jnp.concatenate([meta, x_prompt, jnp.zeros((bp, lpad - lp, d), F32)], axis=1)
    x = jnp.concatenate([xp.reshape(row_s, d), x_sample.reshape(n_s, d),
                         jnp.zeros((tt - row_s - n_s, d), F32)], axis=0)

    def prompt_rows(a):
        return a[:row_s].reshape(bp, lpad, -1)[:, :lp]

    def sample_rows(a):
        return a[row_s:row_s + n_s].reshape(bs, ts, -1)

    outs = {name: [] for name in ("sb_kp", "sb_vp", "sb_ks", "sb_vs", "df_kp", "df_vp", "df_ks", "df_vs",
                                  "cv_p", "cv_s")}
    att_plan = [(True, DH ** -0.5), (True, 1.0), (True, 1.0)]
    for i in range(depth):
        kind, j = i % N_MIXERS, i // N_MIXERS
        w_in = w_mix_in[i].astype(BF16)
        if kind == 1:
            gb, gc, hv = norm_matmul(x, norm_mix[i], w_in, [(True, None)] * 3)
            st = state_conv[j].astype(F32)
            zero = jnp.zeros((bs, ts, d), F32)
            prev1 = zero.at[:, 0].set(st[:, 1]).reshape(n_s, d)
            prev2 = zero.at[:, 0].set(st[:, 0]).at[:, 1].set(st[:, 1]).reshape(n_s, d)
            o, u = gated_conv(gb, gc, hv, prev1, prev2, conv_w[j], lpad, row_s, ts)
            outs["cv_p"].append(prompt_rows(u)[:, -(conv_w.shape[1] - 1):])
            outs["cv_s"].append(sample_rows(u)[:, -(conv_w.shape[1] - 1):])
        else:
            q, q_bf, k, k_bf, v, v_bf = norm_matmul(x, norm_mix[i], w_in, att_plan)
            o = jnp.zeros((tt, d), F32)
            if kind == 0:
                o = sb_prompt_attention(o, q_bf, k_bf, v_bf, bp, lpad)
                o = sample_attention("sb", o, page_table, q, k, v, cache_k_sb, cache_v_sb, j, row_s, bs)
                pre, hk, hv_ = "sb", (H_SB, DH), (H_SB, DH)
            else:
                lam_init = 0.8 - 0.6 * math.exp(-0.3 * i)
                lam4 = jnp.stack([diff_lambda_q1[j], diff_lambda_k1[j], diff_lambda_q2[j],
                                  diff_lambda_k2[j]]).astype(F32)
                o = diff_prompt_attention(o, lam4, diff_subln[j], q_bf, k_bf, v_bf, bp, lpad, lam_init)
                o = sample_attention("diff", o, page_table, q, k, v, cache_k_diff, cache_v_diff, j, row_s,
                                     bs, lam4=lam4, subln=diff_subln[j], lam_init=lam_init)
                pre, hk, hv_ = "df", (2 * H_DIFF, DH), (H_DIFF, 2 * DH)
            outs[pre + "_kp"].append(prompt_rows(k).reshape(bp, lp, *hk))
            outs[pre + "_vp"].append(prompt_rows(v).reshape(bp, lp, *hv_))
            outs[pre + "_ks"].append(sample_rows(k).reshape(bs, ts, *hk))
            outs[pre + "_vs"].append(sample_rows(v).reshape(bs, ts, *hv_))
        x = matmul_residual(o, w_mix_out[i].astype(BF16), x)

        m = i // 2
        if i % 2 == 0:
            x = dense_ffn(x, norm_ffn[i], w_dense_gate[m].astype(BF16), w_dense_up[m].astype(BF16),
                          w_dense_down[m].astype(BF16))
        else:
            x = moe_ffn(x, norm_ffn[i], w_router[m], w_exp_gate[m].astype(BF16),
                        w_exp_up[m].astype(BF16), w_exp_down[m].astype(BF16))

    y = final_norm(x, norm_final)
    y_prompt = prompt_rows(y)[:, N_META:]
    y_sample = sample_rows(y)
    return (y_prompt, y_sample,
            jnp.stack(outs["sb_kp"]), jnp.stack(outs["sb_vp"]), jnp.stack(outs["sb_ks"]),
            jnp.stack(outs["sb_vs"]), jnp.stack(outs["df_kp"]), jnp.stack(outs["df_vp"]),
            jnp.stack(outs["df_ks"]), jnp.stack(outs["df_vs"]),
            jnp.stack(outs["cv_p"]), jnp.stack(outs["cv_s"]))
```

```python
import functools
import math

import jax
import jax.numpy as jnp
from jax import lax
from jax.experimental import pallas as pl
from jax.experimental.pallas import tpu as pltpu

F32 = jnp.float32
BF16 = jnp.bfloat16

D_MODEL = 1024
N_META = 16
N_MIXERS = 3
H_SB = 16
DH = 64
H_DIFF = 8
N_EXPERTS = 8
EPS = 1e-6
LANES = 128
SUBLANES = 8
VMEM_LIMIT = 56 * 1024 * 1024

ATT_BLK = 256
ROW_TILE = 512
FFN_TILE = 896
FFN_CHUNK = 512
MOE_CHUNK = 896
MOE_ROWS = 256
PAGES_PER_STEP = 4
ATT_PAIRS = 4
NEG = -1e30


def _cp(sem, vmem=VMEM_LIMIT):
    return pltpu.CompilerParams(dimension_semantics=sem, vmem_limit_bytes=vmem)


def _rms(x, g):
    ms = jnp.mean(x * x, axis=-1, keepdims=True)
    return x * lax.rsqrt(ms + EPS) * g


def _dot(a, b):
    return jnp.dot(a, b, preferred_element_type=F32)


def _dot_nt(a, b):
    return lax.dot_general(a, b, (((1,), (1,)), ((), ())), preferred_element_type=F32)


def _split_bf16(x):
    hi = x.astype(BF16)
    lo = (x - hi.astype(F32)).astype(BF16)
    return hi, lo


def _silu(g):
    return g / (1.0 + jnp.exp(-g))


def _norm_mm_kernel(x_ref, g_ref, w_ref, *out_refs, plan):
    xn = _rms(x_ref[...], g_ref[...]).astype(BF16)
    oi = 0
    for c, (want_f32, bf_scale) in enumerate(plan):
        y = _dot(xn, w_ref[:, c * D_MODEL:(c + 1) * D_MODEL])
        if want_f32:
            out_refs[oi][...] = y
            oi += 1
        if bf_scale is not None:
            out_refs[oi][...] = (y * bf_scale).astype(BF16)
            oi += 1


def norm_matmul(x, g, w_bf, plan, tm=ROW_TILE):
    tt, d = x.shape
    n = w_bf.shape[1]
    out_shape, out_specs = [], []
    for want_f32, bf_scale in plan:
        for dt in ([F32] if want_f32 else []) + ([BF16] if bf_scale is not None else []):
            out_shape.append(jax.ShapeDtypeStruct((tt, d), dt))
            out_specs.append(pl.BlockSpec((tm, d), lambda i: (i, 0)))
    return pl.pallas_call(
        functools.partial(_norm_mm_kernel, plan=tuple(plan)),
        grid=(tt // tm,),
        in_specs=[pl.BlockSpec((tm, d), lambda i: (i, 0)),
                  pl.BlockSpec((1, d), lambda i: (0, 0)),
                  pl.BlockSpec((d, n), lambda i: (0, 0))],
        out_specs=out_specs,
        out_shape=out_shape,
        compiler_params=_cp(("parallel",)),
        name="norm_matmul",
    )(x, g.reshape(1, d), w_bf)


def _mm_res_kernel(o_ref, w_ref, x_ref, out_ref):
    out_ref[...] = x_ref[...] + _dot(o_ref[...].astype(BF16), w_ref[...])


def matmul_residual(o, w_bf, x, tm=ROW_TILE):
    tt, d = x.shape
    return pl.pallas_call(
        _mm_res_kernel,
        grid=(tt // tm,),
        in_specs=[pl.BlockSpec((tm, d), lambda i: (i, 0)),
                  pl.BlockSpec((d, d), lambda i: (0, 0)),
                  pl.BlockSpec((tm, d), lambda i: (i, 0))],
        out_specs=pl.BlockSpec((tm, d), lambda i: (i, 0)),
        out_shape=jax.ShapeDtypeStruct((tt, d), F32),
        compiler_params=_cp(("parallel",)),
        name="matmul_residual",
    )(o, w_bf, x)


def _ffn_kernel(x_ref, g_ref, wg_ref, wu_ref, wd_ref, out_ref, xn_ref, acc_ref):
    f = pl.program_id(1)

    @pl.when(f == 0)
    def _():
        x = x_ref[...]
        xn_ref[...] = _rms(x, g_ref[...]).astype(BF16)
        acc_ref[...] = x

    xn = xn_ref[...]
    a = (_silu(_dot(xn, wg_ref[...])) * _dot(xn, wu_ref[...])).astype(BF16)
    acc_ref[...] += _dot(a, wd_ref[...])

    @pl.when(f == pl.num_programs(1) - 1)
    def _():
        out_ref[...] = acc_ref[...]


def dense_ffn(x, g, wg, wu, wd, tm=FFN_TILE, tf=FFN_CHUNK):
    tt, d = x.shape
    dff = wg.shape[1]
    return pl.pallas_call(
        _ffn_kernel,
        grid=(tt // tm, dff // tf),
        in_specs=[pl.BlockSpec((tm, d), lambda i, f: (i, 0)),
                  pl.BlockSpec((1, d), lambda i, f: (0, 0)),
                  pl.BlockSpec((d, tf), lambda i, f: (0, f)),
                  pl.BlockSpec((d, tf), lambda i, f: (0, f)),
                  pl.BlockSpec((tf, d), lambda i, f: (f, 0))],
        out_specs=pl.BlockSpec((tm, d), lambda i, f: (i, 0)),
        out_shape=jax.ShapeDtypeStruct((tt, d), F32),
        scratch_shapes=[pltpu.VMEM((tm, d), BF16), pltpu.VMEM((tm, d), F32)],
        compiler_params=_cp(("parallel", "arbitrary")),
        name="dense_ffn",
    )(x, g.reshape(1, d), wg, wu, wd)


def _router_kernel(x_ref, g_ref, wr_ref, hn_ref, slot_ref, gate_ref, cnt_ref, *, tm):
    xn = _rms(x_ref[...], g_ref[...])
    hn_ref[...] = xn.astype(BF16)
    xh, xl = _split_bf16(xn)
    wh, wl = _split_bf16(wr_ref[...])
    logits = _dot(xh, wh) + _dot(xh, wl) + _dot(xl, wh)
    lane = lax.broadcasted_iota(jnp.int32, (tm, LANES), 1).astype(F32)
    logits = jnp.where(lane < N_EXPERTS, logits, NEG)
    m1 = jnp.max(logits, axis=1, keepdims=True)
    i1 = jnp.min(jnp.where(logits == m1, lane, float(LANES)), axis=1, keepdims=True)
    sel1 = lane == i1
    rest = jnp.where(sel1, NEG, logits)
    m2 = jnp.max(rest, axis=1, keepdims=True)
    i2 = jnp.min(jnp.where(rest == m2, lane, float(LANES)), axis=1, keepdims=True)
    sel2 = lane == i2
    e = jnp.exp(m2 - m1)
    gate_ref[...] = jnp.where(sel1, 1.0 / (1.0 + e), 0.0) + jnp.where(sel2, e / (1.0 + e), 0.0)
    sel = jnp.where(sel1 | sel2, 1.0, 0.0)
    r = lax.broadcasted_iota(jnp.int32, (tm, tm), 0)
    c = lax.broadcasted_iota(jnp.int32, (tm, tm), 1)
    before = jnp.where(r > c, 1.0, 0.0).astype(BF16)
    rank = _dot(before, sel.astype(BF16))
    slot_ref[...] = jnp.where(sel > 0.5, rank, -1.0)
    cnt_ref[...] = jnp.broadcast_to(jnp.sum(sel, axis=0, keepdims=True), (SUBLANES, LANES))


def router(x, g, w_router, tm):
    tt, d = x.shape
    nt = tt // tm
    wr = jnp.zeros((d, LANES), F32).at[:, :N_EXPERTS].set(w_router)
    return pl.pallas_call(
        functools.partial(_router_kernel, tm=tm),
        grid=(nt,),
        in_specs=[pl.BlockSpec((tm, d), lambda i: (i, 0)),
                  pl.BlockSpec((1, d), lambda i: (0, 0)),
                  pl.BlockSpec((d, LANES), lambda i: (0, 0))],
        out_specs=[pl.BlockSpec((tm, d), lambda i: (i, 0)),
                   pl.BlockSpec((tm, LANES), lambda i: (i, 0)),
                   pl.BlockSpec((tm, LANES), lambda i: (i, 0)),
                   pl.BlockSpec((None, SUBLANES, LANES), lambda i: (i, 0, 0))],
        out_shape=[jax.ShapeDtypeStruct((tt, d), BF16),
                   jax.ShapeDtypeStruct((tt, LANES), F32),
                   jax.ShapeDtypeStruct((tt, LANES), F32),
                   jax.ShapeDtypeStruct((nt, SUBLANES, LANES), F32)],
        compiler_params=_cp(("parallel",)),
        name="router",
    )(x, g.reshape(1, d), wr)


def _moe_kernel(cnt_ref, hn_ref, x_ref, slotc_ref, slotr_ref, gater_ref, wg_ref, wu_ref, wd_ref,
                out_ref, xc_ref, yacc_ref, *, tm, rows):
    i, e, f = pl.program_id(0), pl.program_id(1), pl.program_id(2)
    nf = pl.num_programs(2)
    nb = (cnt_ref[i * N_EXPERTS + e] + rows - 1) // rows
    slot_r = slotr_ref[...]

    def one_hot_rows(rb):
        rid = (lax.broadcasted_iota(jnp.int32, (rows, tm), 0) + rb * rows).astype(F32)
        return slot_r == rid

    @pl.when((e == 0) & (f == 0))
    def _():
        out_ref[...] = x_ref[...]

    @pl.when(f == 0)
    def _():
        def body(rb, carry):
            p = jnp.where(one_hot_rows(rb), 1.0, 0.0).astype(BF16)
            r0 = pl.multiple_of(rb * rows, rows)
            xc_ref[pl.ds(r0, rows), :] = _dot(p, hn_ref[...]).astype(BF16)
            return carry
        lax.fori_loop(0, nb, body, 0)

    def ffn_body(rb, carry):
        r0 = pl.multiple_of(rb * rows, rows)
        xs = xc_ref[pl.ds(r0, rows), :]
        a = (_silu(_dot(xs, wg_ref[...])) * _dot(xs, wu_ref[...])).astype(BF16)
        y = _dot(a, wd_ref[...])

        @pl.when(f == 0)
        def _():
            yacc_ref[pl.ds(r0, rows), :] = y

        @pl.when(f > 0)
        def _():
            yacc_ref[pl.ds(r0, rows), :] += y
        return carry
    lax.fori_loop(0, nb, ffn_body, 0)

    @pl.when(f == nf - 1)
    def _():
        lane = lax.broadcasted_iota(jnp.int32, (tm, LANES), 1)
        slot_c = jnp.sum(jnp.where(lane == e, slotc_ref[...], 0.0), axis=1, keepdims=True)
        gate_r = gater_ref[...]

        def body(rb, carry):
            r0 = pl.multiple_of(rb * rows, rows)
            gc = jnp.sum(jnp.where(one_hot_rows(rb), gate_r, 0.0), axis=1, keepdims=True)
            ys = (yacc_ref[pl.ds(r0, rows), :] * gc).astype(BF16)
            cid = (lax.broadcasted_iota(jnp.int32, (tm, rows), 1) + rb * rows).astype(F32)
            pt = jnp.where(slot_c == cid, 1.0, 0.0).astype(BF16)
            out_ref[...] += _dot(pt, ys)
            return carry
        lax.fori_loop(0, nb, body, 0)


def moe_ffn(x, g, w_router, wg, wu, wd, tm=FFN_TILE, tf=MOE_CHUNK, rows=MOE_ROWS):
    tt, d = x.shape
    nt = tt // tm
    dff = wg.shape[2]
    hn, slot, gate, cnt = router(x, g, w_router, tm)
    cnt_i = cnt[:, 0, :N_EXPERTS].astype(jnp.int32).reshape(nt * N_EXPERTS)

    def to_rows(a):
        a = a[:, :N_EXPERTS].reshape(nt, tm, N_EXPERTS)
        return a.transpose(0, 2, 1).reshape(nt, N_EXPERTS, 1, tm)

    grid_spec = pltpu.PrefetchScalarGridSpec(
        num_scalar_prefetch=1,
        grid=(nt, N_EXPERTS, dff // tf),
        in_specs=[pl.BlockSpec((tm, d), lambda i, e, f, c: (i, 0)),
                  pl.BlockSpec((tm, d), lambda i, e, f, c: (i, 0)),
                  pl.BlockSpec((tm, LANES), lambda i, e, f, c: (i, 0)),
                  pl.BlockSpec((None, None, 1, tm), lambda i, e, f, c: (i, e, 0, 0)),
                  pl.BlockSpec((None, None, 1, tm), lambda i, e, f, c: (i, e, 0, 0)),
                  pl.BlockSpec((None, d, tf), lambda i, e, f, c: (e, 0, f)),
                  pl.BlockSpec((None, d, tf), lambda i, e, f, c: (e, 0, f)),
                  pl.BlockSpec((None, tf, d), lambda i, e, f, c: (e, f, 0))],
        out_specs=pl.BlockSpec((tm, d), lambda i, e, f, c: (i, 0)),
        scratch_shapes=[pltpu.VMEM((_round_up(tm, rows), d), BF16), pltpu.VMEM((_round_up(tm, rows), d), F32)],
    )
    return pl.pallas_call(
        functools.partial(_moe_kernel, tm=tm, rows=rows),
        grid_spec=grid_spec,
        out_shape=jax.ShapeDtypeStruct((tt, d), F32),
        compiler_params=_cp(("parallel", "arbitrary", "arbitrary")),
        name="moe_ffn",
    )(cnt_i, hn, x, slot, to_rows(slot), to_rows(gate), wg, wu, wd)


def _after_matrix(n):
    r = lax.broadcasted_iota(jnp.int32, (n, n), 0)
    c = lax.broadcasted_iota(jnp.int32, (n, n), 1)
    return jnp.where(r > c, 1.0, 0.0).astype(BF16)


def _sb_logs(z, valid):
    t = jnp.log(1.0 + jnp.exp(-jnp.abs(z)))
    log_beta = jnp.minimum(z, 0.0) - t
    log_keep = log_beta - z
    if valid is not None:
        log_keep = jnp.where(valid, log_keep, 0.0)
    return log_beta, log_keep


def _sb_keep_after(log_keep, after):
    hi, lo = _split_bf16(log_keep)
    return _dot(hi, after) + _dot(lo, after)


def _sb_weights(log_beta, log_keep, keep_after, carry, valid):
    w = jnp.exp(log_beta + keep_after + carry)
    if valid is not None:
        w = jnp.where(valid, w, 0.0)
    return w, carry + keep_after[:, 0:1] + log_keep[:, 0:1]


def _sb_block(z, valid, after, carry):
    log_beta, log_keep = _sb_logs(z, valid)
    return _sb_weights(log_beta, log_keep, _sb_keep_after(log_keep, after), carry, valid)


def _head_pair(q2):
    lane = lax.broadcasted_iota(jnp.int32, q2.shape, 1)
    zero = jnp.zeros_like(q2)
    return jnp.where(lane < DH, q2, zero), jnp.where(lane >= DH, q2, zero)


def _sb_prompt_kernel(o_hbm_ref, q_ref, k_ref, v_ref, o_ref, acc_ref, car_ref, *, blk, n_pair):
    del o_hbm_ref
    qi = pl.program_id(2)
    n_head = 2 * n_pair
    qh = []
    for p in range(n_pair):
        qh += list(_head_pair(q_ref[:, p * LANES:(p + 1) * LANES]))
    after = _after_matrix(blk)
    acc_ref[...] = jnp.zeros_like(acc_ref)
    car_ref[...] = jnp.zeros_like(car_ref)
    row = lax.broadcasted_iota(jnp.int32, (blk, blk), 0)
    col = lax.broadcasted_iota(jnp.int32, (blk, blk), 1)

    def block(kj, valid):
        k0 = pl.multiple_of(kj * blk, blk)
        kk = [k_ref[pl.ds(k0, blk), p * LANES:(p + 1) * LANES] for p in range(n_pair)]
        vv = [v_ref[pl.ds(k0, blk), p * LANES:(p + 1) * LANES] for p in range(n_pair)]
        zs = [_dot_nt(qh[h], kk[h // 2]) for h in range(n_head)]
        logs = [_sb_logs(z, valid) for z in zs]
        keep_after = [_sb_keep_after(lk, after) for _, lk in logs]
        ws = []
        for h in range(n_head):
            w, car_ref[h] = _sb_weights(logs[h][0], logs[h][1], keep_after[h], car_ref[h], valid)
            ws.append(w.astype(BF16))
        for h in range(n_head):
            acc_ref[h] += _dot(ws[h], vv[h // 2])

    block(qi, col < row)

    def body(s, carry):
        block(qi - 1 - s, None)
        return carry
    lax.fori_loop(0, qi, body, 0)
    lane = lax.broadcasted_iota(jnp.int32, (blk, LANES), 1)
    for p in range(n_pair):
        o_ref[:, p * LANES:(p + 1) * LANES] = jnp.where(lane < DH, acc_ref[2 * p], acc_ref[2 * p + 1])


def sb_prompt_attention(o_init, q_bf, k_bf, v_bf, n_batch, lpad, blk=ATT_BLK, n_pair=ATT_PAIRS):
    tt, d = q_bf.shape
    nq = lpad // blk
    wid = n_pair * LANES
    return pl.pallas_call(
        functools.partial(_sb_prompt_kernel, blk=blk, n_pair=n_pair),
        grid=(n_batch, d // wid, nq),
        in_specs=[pl.BlockSpec(memory_space=pl.ANY),
                  pl.BlockSpec((blk, wid), lambda b, h, i: (b * nq + i, h)),
                  pl.BlockSpec((lpad, wid), lambda b, h, i: (b, h)),
                  pl.BlockSpec((lpad, wid), lambda b, h, i: (b, h))],
        out_specs=pl.BlockSpec((blk, wid), lambda b, h, i: (b * nq + i, h)),
        out_shape=jax.ShapeDtypeStruct((tt, d), F32),
        scratch_shapes=[pltpu.VMEM((2 * n_pair, blk, LANES), F32), pltpu.VMEM((2 * n_pair, blk, 1), F32)],
        input_output_aliases={0: 0},
        compiler_params=_cp(("parallel", "parallel", "arbitrary")),
        name="sb_prompt_attention",
    )(o_init, q_bf, k_bf, v_bf)


def _lambda(lam_ref, lam_init):
    a = jnp.sum(lam_ref[0:1, :] * lam_ref[1:2, :], axis=1, keepdims=True)
    b = jnp.sum(lam_ref[2:3, :] * lam_ref[3:4, :], axis=1, keepdims=True)
    return jnp.exp(a) - jnp.exp(b) + lam_init


def _diff_prompt_kernel(o_hbm_ref, lam_ref, g_ref, q_ref, k_ref, v_ref, o_ref, m_ref, l_ref, acc_ref,
                        *, blk, n_pair, lam_init):
    del o_hbm_ref
    qi = pl.program_id(2)
    n_map = 2 * n_pair
    qh = []
    for p in range(n_pair):
        qh += list(_head_pair(q_ref[:, p * LANES:(p + 1) * LANES]))
    m_ref[...] = jnp.full_like(m_ref, NEG)
    l_ref[...] = jnp.zeros_like(l_ref)
    acc_ref[...] = jnp.zeros_like(acc_ref)
    row = lax.broadcasted_iota(jnp.int32, (blk, blk), 0)
    col = lax.broadcasted_iota(jnp.int32, (blk, blk), 1)
    ones = jnp.ones((blk, LANES), BF16)

    def block(kj, valid):
        k0 = pl.multiple_of(kj * blk, blk)
        kk = [k_ref[pl.ds(k0, blk), p * LANES:(p + 1) * LANES] for p in range(n_pair)]
        vv = [v_ref[pl.ds(k0, blk), p * LANES:(p + 1) * LANES] for p in range(n_pair)]
        ss = [_dot_nt(qh[h], kk[h // 2]) for h in range(n_map)]
        if valid is not None:
            ss = [jnp.where(valid, s, NEG) for s in ss]
        ps, alphas = [], []
        for h in range(n_map):
            m_old = m_ref[h]
            m_new = jnp.maximum(m_old, jnp.max(ss[h], axis=1, keepdims=True))
            alphas.append(jnp.exp(m_old - m_new))
            ps.append(jnp.exp(ss[h] - jnp.concatenate([m_new] * (blk // LANES), axis=1)).astype(BF16))
            m_ref[h] = m_new
        v1 = [jnp.concatenate([v, ones], axis=1) for v in vv]
        for h in range(n_map):
            pv = _dot(ps[h], v1[h // 2])
            acc_ref[h] = alphas[h] * acc_ref[h] + pv[:, :LANES]
            l_ref[h] = alphas[h] * l_ref[h] + pv[:, LANES:]

    block(qi, col <= row)

    def body(s, carry):
        block(s, None)
        return carry
    lax.fori_loop(0, qi, body, 0)
    lam = _lambda(lam_ref, lam_init)
    for p in range(n_pair):
        o = acc_ref[2 * p] / l_ref[2 * p] - lam * (acc_ref[2 * p + 1] / l_ref[2 * p + 1])
        o_ref[:, p * LANES:(p + 1) * LANES] = _rms(o, g_ref[...]) * (1.0 - lam_init)


def diff_prompt_attention(o_init, lam4, subln, q_bf, k_bf, v_bf, n_batch, lpad, lam_init, blk=ATT_BLK,
                          n_pair=ATT_PAIRS):
    tt, d = q_bf.shape
    nq = lpad // blk
    wid = n_pair * LANES
    return pl.pallas_call(
        functools.partial(_diff_prompt_kernel, blk=blk, n_pair=n_pair, lam_init=lam_init),
        grid=(n_batch, d // wid, nq),
        in_specs=[pl.BlockSpec(memory_space=pl.ANY),
                  pl.BlockSpec((4, DH), lambda b, h, i: (0, 0)),
                  pl.BlockSpec((1, LANES), lambda b, h, i: (0, 0)),
                  pl.BlockSpec((blk, wid), lambda b, h, i: (b * nq + i, h)),
                  pl.BlockSpec((lpad, wid), lambda b, h, i: (b, h)),
                  pl.BlockSpec((lpad, wid), lambda b, h, i: (b, h))],
        out_specs=pl.BlockSpec((blk, wid), lambda b, h, i: (b * nq + i, h)),
        out_shape=jax.ShapeDtypeStruct((tt, d), F32),
        scratch_shapes=[pltpu.VMEM((2 * n_pair, blk, LANES), F32)] * 3,
        input_output_aliases={0: 0},
        compiler_params=_cp(("parallel", "parallel", "arbitrary")),
        name="diff_prompt_attention",
    )(o_init, lam4, subln.reshape(1, LANES), q_bf, k_bf, v_bf)


def _block_diag_mask(n_heads_rows, lane_group):
    shape = (n_heads_rows * SUBLANES, D_MODEL)
    rh = lax.broadcasted_iota(jnp.int32, shape, 0) // SUBLANES
    lh = lax.broadcasted_iota(jnp.int32, shape, 1) // lane_group
    return rh, lh


def _page_tokens(ref):
    n_heads = ref.shape[0] // LANES
    heads = [ref[pl.ds(h, LANES, stride=n_heads), :] for h in range(n_heads)]
    return jnp.concatenate(heads, axis=1).astype(BF16)


def _page_values(w, v_ref, v_by_width):
    if v_by_width:
        return _dot_nt(w, v_ref[...].astype(BF16))
    return _dot(w, _page_tokens(v_ref))


def _new_token_masks():
    key = lax.broadcasted_iota(jnp.int32, (LANES, LANES), 1)
    tq = lax.broadcasted_iota(jnp.int32, (LANES, LANES), 0) % SUBLANES
    return key, tq


def _sample_setup(q_ref, kn_ref, vn_ref, qbd_ref, kpad_ref, vpad_ref):
    q = q_ref[...] * (DH ** -0.5)
    qt = jnp.broadcast_to(q[None], (H_SB, SUBLANES, D_MODEL)).reshape(H_SB * SUBLANES, D_MODEL)
    rh, lh = _block_diag_mask(H_SB, DH)
    qbd_ref[...] = jnp.where(rh == lh, qt, 0.0).astype(BF16)
    kpad_ref[...] = jnp.zeros_like(kpad_ref)
    vpad_ref[...] = jnp.zeros_like(vpad_ref)
    kpad_ref[0:SUBLANES, :] = kn_ref[...]
    vpad_ref[0:SUBLANES, :] = vn_ref[...]


def _sb_sample_kernel(pt_ref, o_hbm_ref, q_ref, kn_ref, vn_ref, *rest, n_pg, v_by_width):
    del pt_ref, o_hbm_ref
    k_refs, v_refs = rest[:n_pg], rest[n_pg:2 * n_pg]
    o_ref, qbd_ref, acc_ref, car_ref, kpad_ref, vpad_ref = rest[2 * n_pg:]
    st = pl.program_id(1)
    after = _after_matrix(LANES)

    @pl.when(st == 0)
    def _():
        _sample_setup(q_ref, kn_ref, vn_ref, qbd_ref, kpad_ref, vpad_ref)
        key, tq = _new_token_masks()
        z = _dot_nt(qbd_ref[...], kpad_ref[...].astype(BF16))
        w, car_ref[...] = _sb_block(z, key < tq, after, jnp.zeros(car_ref.shape, F32))
        acc_ref[...] = _dot(w.astype(BF16), vpad_ref[...].astype(BF16))

    @pl.when(st > 0)
    def _():
        order = list(reversed(range(n_pg)))
        qbd = qbd_ref[...]
        zs = [_dot(qbd, k_refs[p][...].astype(BF16)) for p in order]
        logs = [_sb_logs(z, None) for z in zs]
        keep_after = [_sb_keep_after(lk, after) for _, lk in logs]
        car = car_ref[...]
        ws = []
        for i in range(n_pg):
            w, car = _sb_weights(logs[i][0], logs[i][1], keep_after[i], car, None)
            ws.append(w.astype(BF16))
        car_ref[...] = car
        acc = acc_ref[...]
        for i, p in enumerate(order):
            acc = acc + _page_values(ws[i], v_refs[p], v_by_width)
        acc_ref[...] = acc

    @pl.when(st == pl.num_programs(1) - 1)
    def _():
        rh, lh = _block_diag_mask(H_SB, DH)
        a = jnp.where(rh == lh, acc_ref[...], 0.0).reshape(H_SB, SUBLANES, D_MODEL)
        o_ref[...] = jnp.sum(a, axis=0)


def _diff_sample_kernel(pt_ref, o_hbm_ref, lam_ref, g_ref, q_ref, kn_ref, vn_ref, *rest, n_pg, v_by_width,
                        lam_init):
    del pt_ref, o_hbm_ref
    k_refs, v_refs = rest[:n_pg], rest[n_pg:2 * n_pg]
    o_ref, qbd_ref, acc_ref, m_ref, l_ref, kpad_ref, vpad_ref = rest[2 * n_pg:]
    st = pl.program_id(1)

    @pl.when(st == 0)
    def _():
        _sample_setup(q_ref, kn_ref, vn_ref, qbd_ref, kpad_ref, vpad_ref)
        key, tq = _new_token_masks()
        s = jnp.where(key <= tq, _dot_nt(qbd_ref[...], kpad_ref[...].astype(BF16)), NEG)
        m = jnp.max(s, axis=1, keepdims=True)
        p = jnp.exp(s - m)
        m_ref[...] = m
        l_ref[...] = jnp.sum(p, axis=1, keepdims=True)
        acc_ref[...] = _dot(p.astype(BF16), vpad_ref[...].astype(BF16))

    @pl.when(st > 0)
    def _():
        qbd = qbd_ref[...]
        ss = [_dot(qbd, k_refs[p][...].astype(BF16)) for p in range(n_pg)]
        top = ss[0]
        for s in ss[1:]:
            top = jnp.maximum(top, s)
        m_old = m_ref[...]
        m_new = jnp.maximum(m_old, jnp.max(top, axis=1, keepdims=True))
        alpha = jnp.exp(m_old - m_new)
        ps = [jnp.exp(s - m_new) for s in ss]
        tot = ps[0]
        for p in ps[1:]:
            tot = tot + p
        l_ref[...] = alpha * l_ref[...] + jnp.sum(tot, axis=1, keepdims=True)
        m_ref[...] = m_new
        acc = alpha * acc_ref[...]
        for i in range(n_pg):
            acc = acc + _page_values(ps[i].astype(BF16), v_refs[i], v_by_width)
        acc_ref[...] = acc

    @pl.when(st == pl.num_programs(1) - 1)
    def _():
        lam = _lambda(lam_ref, lam_init)
        rh, lh = _block_diag_mask(2 * H_DIFF, 2 * DH)
        coef = jnp.where(rh % 2 == 0, 1.0, -lam)
        a = jnp.where(rh // 2 == lh, acc_ref[...] / l_ref[...] * coef, 0.0)
        o = jnp.sum(a.reshape(2 * H_DIFF, SUBLANES, D_MODEL), axis=0)
        for h in range(H_DIFF):
            sl = slice(h * 2 * DH, (h + 1) * 2 * DH)
            o_ref[:, sl] = _rms(o[:, sl], g_ref[...]) * (1.0 - lam_init)


def sample_attention(kind, o_init, page_table, q, k_new, v_new, cache_k, cache_v, layer, row0, n_seq,
                     lam4=None, subln=None, lam_init=None, n_pg=PAGES_PER_STEP):
    tt, d = q.shape
    n_pages = page_table.shape[1]
    page = cache_k.shape[2]
    assert page == LANES and n_pages % n_pg == 0 and row0 % SUBLANES == 0
    n_grp = n_pages // n_pg
    n_pool = cache_k.shape[1]
    def page_view(c):
        n_layers, _, _, heads, width = c.shape
        if width < LANES:
            return jnp.transpose(c, (0, 1, 3, 4, 2)).reshape(n_layers, n_pool, heads * width, page), True
        return c.reshape(n_layers, n_pool, page * heads, width), False

    ck, k_by_width = page_view(cache_k)
    cv, v_by_width = page_view(cache_v)
    assert k_by_width and ck.shape[2:] == cv.shape[2:] == (d, LANES)
    blk0 = row0 // SUBLANES

    def tok_spec():
        return pl.BlockSpec((SUBLANES, d), lambda s, st, pt: (blk0 + s, 0))

    def page_spec(p):
        if kind == "sb":
            def im(s, st, pt):
                grp = jnp.minimum(n_grp - st, n_grp - 1)
                return (layer, pt[s, grp * n_pg + p], 0, 0)
        else:
            def im(s, st, pt):
                grp = jnp.maximum(st - 1, 0)
                return (layer, pt[s, grp * n_pg + p], 0, 0)
        return pl.BlockSpec((None, None, d, LANES), im)

    rows = H_SB * SUBLANES
    common_scratch = [pltpu.VMEM((rows, d), BF16), pltpu.VMEM((rows, d), F32)]
    pad_scratch = [pltpu.VMEM((LANES, d), F32), pltpu.VMEM((LANES, d), F32)]
    page_specs = [page_spec(p) for p in range(n_pg)] * 2
    page_args = [ck] * n_pg + [cv] * n_pg
    if kind == "sb":
        body = functools.partial(_sb_sample_kernel, n_pg=n_pg, v_by_width=v_by_width)
        extra_specs, extra_args = [], []
        scratch = common_scratch + [pltpu.VMEM((rows, 1), F32)] + pad_scratch
    else:
        body = functools.partial(_diff_sample_kernel, n_pg=n_pg, v_by_width=v_by_width, lam_init=lam_init)
        extra_specs = [pl.BlockSpec((4, DH), lambda s, st, pt: (0, 0)),
                       pl.BlockSpec((1, LANES), lambda s, st, pt: (0, 0))]
        extra_args = [lam4, subln.reshape(1, LANES)]
        scratch = common_scratch + [pltpu.VMEM((rows, 1), F32), pltpu.VMEM((rows, 1), F32)] + pad_scratch
    grid_spec = pltpu.PrefetchScalarGridSpec(
        num_scalar_prefetch=1,
        grid=(n_seq, n_grp + 1),
        in_specs=[pl.BlockSpec(memory_space=pl.ANY)] + extra_specs
                 + [tok_spec(), tok_spec(), tok_spec()] + page_specs,
        out_specs=tok_spec(),
        scratch_shapes=scratch,
    )
    n_in = 1 + len(extra_args)
    return pl.pallas_call(
        body,
        grid_spec=grid_spec,
        out_shape=jax.ShapeDtypeStruct((tt, d), F32),
        input_output_aliases={1: 0},
        compiler_params=_cp(("parallel", "arbitrary")),
        name=kind + "_sample_attention",
    )(page_table, o_init, *extra_args, q, k_new, v_new, *page_args)


def _conv_kernel(gb_ref, gc_ref, hv_ref, hgc_ref, hhv_ref, p1_ref, p2_ref, w_ref, o_ref, u_ref,
                 *, tiles_per_seq, sample_tile, dec_seq):
    i = pl.program_id(0)
    u = gc_ref[...] * hv_ref[...]
    u_ref[...] = u
    row = lax.broadcasted_iota(jnp.int32, u.shape, 0)
    r1 = pltpu.roll(u, 1, axis=0)
    r2 = pltpu.roll(u, 2, axis=0)
    w0, w1, w2 = w_ref[0:1, :], w_ref[1:2, :], w_ref[2:3, :]

    def finish(prev1, prev2):
        o_ref[...] = gb_ref[...] * (w0 * prev2 + w1 * prev1 + w2 * u)

    @pl.when(i == sample_tile)
    def _():
        pos = row % dec_seq
        finish(jnp.where(pos == 0, p1_ref[...], r1), jnp.where(pos < 2, p2_ref[...], r2))

    @pl.when(i != sample_tile)
    def _():
        hu = hgc_ref[...] * hhv_ref[...]
        hu = jnp.where(i % tiles_per_seq == 0, 0.0, hu)
        h7, h6 = hu[7:8, :], hu[6:7, :]
        finish(jnp.where(row == 0, h7, r1), jnp.where(row == 0, h6, jnp.where(row == 1, h7, r2)))


def gated_conv(gb, gc, hv, prev1, prev2, w, lpad, sample_row0, dec_seq, tc=ATT_BLK):
    tt, d = gb.shape
    assert lpad % tc == 0 and sample_row0 % tc == 0 and prev1.shape[0] == tc
    per = tc // SUBLANES

    def tile():
        return pl.BlockSpec((tc, d), lambda i: (i, 0))

    def halo():
        return pl.BlockSpec((SUBLANES, d), lambda i: (jnp.maximum(i * per - 1, 0), 0))

    def whole(r):
        return pl.BlockSpec((r, d), lambda i: (0, 0))

    return pl.pallas_call(
        functools.partial(_conv_kernel, tiles_per_seq=lpad // tc, sample_tile=sample_row0 // tc,
                          dec_seq=dec_seq),
        grid=(tt // tc,),
        in_specs=[tile(), tile(), tile(), halo(), halo(), whole(tc), whole(tc), whole(w.shape[0])],
        out_specs=[tile(), tile()],
        out_shape=[jax.ShapeDtypeStruct((tt, d), F32), jax.ShapeDtypeStruct((tt, d), F32)],
        compiler_params=_cp(("parallel",)),
        name="gated_conv",
    )(gb, gc, hv, gc, hv, prev1, prev2, w)


def _norm_kernel(x_ref, g_ref, o_ref):
    o_ref[...] = _rms(x_ref[...], g_ref[...])


def final_norm(x, g, tm=ROW_TILE):
    tt, d = x.shape
    return pl.pallas_call(
        _norm_kernel,
        grid=(tt // tm,),
        in_specs=[pl.BlockSpec((tm, d), lambda i: (i, 0)), pl.BlockSpec((1, d), lambda i: (0, 0))],
        out_specs=pl.BlockSpec((tm, d), lambda i: (i, 0)),
        out_shape=jax.ShapeDtypeStruct((tt, d), F32),
        compiler_params=_cp(("parallel",)),
        name="final_norm",
    )(x, g.reshape(1, d))


def _round_up(a, b):
    return -(-a // b) * b


def kernel(x_prompt, x_sample, cache_k_sb, cache_v_sb, cache_k_diff, cache_v_diff, state_conv, page_table,
           meta_tokens, norm_mix, w_mix_in, w_mix_out, conv_w, diff_lambda_q1, diff_lambda_k1,
           diff_lambda_q2, diff_lambda_k2, diff_subln, norm_ffn, w_dense_gate, w_dense_up, w_dense_down,
           w_router, w_exp_gate, w_exp_up, w_exp_down, norm_final):
    bp, seq, d = x_prompt.shape
    bs, ts, _ = x_sample.shape
    depth = w_mix_in.shape[0]
    assert d == D_MODEL and ts == SUBLANES and H_SB * ts == LANES
    lp = seq + N_META
    lpad = _round_up(lp, ATT_BLK)
    n_s = bs * ts
    row_s = bp * lpad
    tt = _round_up(row_s + n_s, math.lcm(ROW_TILE, FFN_TILE, ATT_BLK))
    assert n_s == ATT_BLK

    meta = jnp.broadcast_to(meta_tokens[None].astype(F32), (bp, N_META, d))
    xp = jnp.concatenate([meta, x_prompt, jnp.zeros((bp, lpad - lp, d), F32)], axis=1)
    x = jnp.concatenate([xp.reshape(row_s, d), x_sample.reshape(n_s, d),
                         jnp.zeros((tt - row_s - n_s, d), F32)], axis=0)

    def prompt_rows(a):
        return a[:row_s].reshape(bp, lpad, -1)[:, :lp]

    def sample_rows(a):
        return a[row_s:row_s + n_s].reshape(bs, ts, -1)

    outs = {name: [] for name in ("sb_kp", "sb_vp", "sb_ks", "sb_vs", "df_kp", "df_vp", "df_ks", "df_vs",
                                  "cv_p", "cv_s")}
    att_plan = [(True, DH ** -0.5), (True, 1.0), (True, 1.0)]
    for i in range(depth):
        kind, j = i % N_MIXERS, i // N_MIXERS
        w_in = w_mix_in[i].astype(BF16)
        if kind == 1:
            gb, gc, hv = norm_matmul(x, norm_mix[i], w_in, [(True, None)] * 3)
            st = state_conv[j].astype(F32)
            zero = jnp.zeros((bs, ts, d), F32)
            prev1 = zero.at[:, 0].set(st[:, 1]).reshape(n_s, d)
            prev2 = zero.at[:, 0].set(st[:, 0]).at[:, 1].set(st[:, 1]).reshape(n_s, d)
            o, u = gated_conv(gb, gc, hv, prev1, prev2, conv_w[j], lpad, row_s, ts)
            outs["cv_p"].append(prompt_rows(u)[:, -(conv_w.shape[1] - 1):])
            outs["cv_s"].append(sample_rows(u)[:, -(conv_w.shape[1] - 1):])
        else:
            q, q_bf, k, k_bf, v, v_bf = norm_matmul(x, norm_mix[i], w_in, att_plan)
            o = jnp.zeros((tt, d), F32)
            if kind == 0:
                o = sb_prompt_attention(o, q_bf, k_bf, v_bf, bp, lpad)
                o = sample_attention("sb", o, page_table, q, k, v, cache_k_sb, cache_v_sb, j, row_s, bs)
                pre, hk, hv_ = "sb", (H_SB, DH), (H_SB, DH)
            else:
                lam_init = 0.8 - 0.6 * math.exp(-0.3 * i)
                lam4 = jnp.stack([diff_lambda_q1[j], diff_lambda_k1[j], diff_lambda_q2[j],
                                  diff_lambda_k2[j]]).astype(F32)
                o = diff_prompt_attention(o, lam4, diff_subln[j], q_bf, k_bf, v_bf, bp, lpad, lam_init)
                o = sample_attention("diff", o, page_table, q, k, v, cache_k_diff, cache_v_diff, j, row_s,
                                     bs, lam4=lam4, subln=diff_subln[j], lam_init=lam_init)
                pre, hk, hv_ = "df", (2 * H_DIFF, DH), (H_DIFF, 2 * DH)
            outs[pre + "_kp"].append(prompt_rows(k).reshape(bp, lp, *hk))
            outs[pre + "_vp"].append(prompt_rows(v).reshape(bp, lp, *hv_))
            outs[pre + "_ks"].append(sample_rows(k).reshape(bs, ts, *hk))
            outs[pre + "_vs"].append(sample_rows(v).reshape(bs, ts, *hv_))
        x = matmul_residual(o, w_mix_out[i].astype(BF16), x)

        m = i // 2
        if i % 2 == 0:
            x = dense_ffn(x, norm_ffn[i], w_dense_gate[m].astype(BF16), w_dense_up[m].astype(BF16),
                          w_dense_down[m].astype(BF16))
        else:
            x = moe_ffn(x, norm_ffn[i], w_router[m], w_exp_gate[m].astype(BF16),
                        w_exp_up[m].astype(BF16), w_exp_down[m].astype(BF16))

    y = final_norm(x, norm_final)
    y_prompt = prompt_rows(y)[:, N_META:]
    y_sample = sample_rows(y)
    return (y_prompt, y_sample,
            jnp.stack(outs["sb_kp"]), jnp.stack(outs["sb_vp"]), jnp.stack(outs["sb_ks"]),
            jnp.stack(outs["sb_vs"]), jnp.stack(outs["df_kp"]), jnp.stack(outs["df_vp"]),
            jnp.stack(outs["df_ks"]), jnp.stack(outs["df_vs"]),
            jnp.stack(outs["cv_p"]), jnp.stack(outs["cv_s"]))
```

```python
import functools
import math

import jax
import jax.numpy as jnp
from jax import lax
from jax.experimental import pallas as pl
from jax.experimental.pallas import tpu as pltpu

F32 = jnp.float32
BF16 = jnp.bfloat16

D_MODEL = 1024
N_META = 16
N_MIXERS = 3
H_SB = 16
DH = 64
H_DIFF = 8
N_EXPERTS = 8
EPS = 1e-6
LANES = 128
SUBLANES = 8
VMEM_LIMIT = 56 * 1024 * 1024

ATT_BLK = 256
ROW_TILE = 512
FFN_TILE = 896
FFN_CHUNK = 512
MOE_TILE = 1792
MOE_CHUNK = 896
MOE_ROWS = 512
MOE_VMEM_LIMIT = 60 * 1024 * 1024
PAGES_PER_STEP = 8
ATT_PAIRS = 4
NEG = -1e30
LOG2E = 1.4426950408889634


def _cp(sem, vmem=VMEM_LIMIT):
    return pltpu.CompilerParams(dimension_semantics=sem, vmem_limit_bytes=vmem)


def _rms(x, g):
    ms = jnp.mean(x * x, axis=-1, keepdims=True)
    return x * lax.rsqrt(ms + EPS) * g


def _dot(a, b):
    return jnp.dot(a, b, preferred_element_type=F32)


def _dot_nt(a, b):
    return lax.dot_general(a, b, (((1,), (1,)), ((), ())), preferred_element_type=F32)


def _split_bf16(x):
    hi = x.astype(BF16)
    lo = (x - hi.astype(F32)).astype(BF16)
    return hi, lo


def _mm(a, b, precise, nt=False):
    dot = _dot_nt if nt else _dot
    if not precise:
        return dot(a.astype(BF16), b.astype(BF16))
    ah, al = _split_bf16(a)
    bh, bl = _split_bf16(b)
    return dot(ah, bh) + dot(ah, bl) + dot(al, bh)


def _silu(g):
    return g / (1.0 + jnp.exp(-g))


def _norm_mm_kernel(x_ref, d_ref, g_ref, w_ref, *out_refs, plan, has_delta, precise):
    x = x_ref[...]
    oi = 0
    if has_delta:
        x = x + d_ref[...]
        out_refs[0][...] = x
        oi = 1
    xn = _rms(x, g_ref[...])
    if not precise:
        xn = xn.astype(BF16)
    for c, (want_f32, bf_scale) in enumerate(plan):
        y = _mm(xn, w_ref[:, c * D_MODEL:(c + 1) * D_MODEL], precise)
        if want_f32:
            out_refs[oi][...] = y
            oi += 1
        if bf_scale is not None:
            out_refs[oi][...] = (y * bf_scale).astype(BF16)
            oi += 1


def _delta_operand(x, delta, tm):
    d = x.shape[1]
    if delta is None:
        return x, pl.BlockSpec((SUBLANES, d), lambda i: (0, 0))
    return delta, pl.BlockSpec((tm, d), lambda i: (i, 0))


def norm_matmul(x, g, w, plan, delta=None, tm=ROW_TILE, precise=False):
    tt, d = x.shape
    tm = min(tm, tt)
    n = w.shape[1]
    dts = [F32] if delta is not None else []
    for want_f32, bf_scale in plan:
        dts += ([F32] if want_f32 else []) + ([BF16] if bf_scale is not None else [])
    d_arg, d_spec = _delta_operand(x, delta, tm)
    return pl.pallas_call(
        functools.partial(_norm_mm_kernel, plan=tuple(plan), has_delta=delta is not None, precise=precise),
        grid=(tt // tm,),
        in_specs=[pl.BlockSpec((tm, d), lambda i: (i, 0)),
                  d_spec,
                  pl.BlockSpec((1, d), lambda i: (0, 0)),
                  pl.BlockSpec((d, n), lambda i: (0, 0))],
        out_specs=[pl.BlockSpec((tm, d), lambda i: (i, 0)) for _ in dts],
        out_shape=[jax.ShapeDtypeStruct((tt, d), dt) for dt in dts],
        compiler_params=_cp(("parallel",)),
        name="norm_matmul",
    )(x, d_arg, g.reshape(1, d), w)


def _mm_res_kernel(o_ref, w_ref, x_ref, out_ref, *, precise):
    out_ref[...] = x_ref[...] + _mm(o_ref[...], w_ref[...], precise)


def matmul_residual(o, w, x, tm=ROW_TILE, precise=False):
    tt, d = x.shape
    tm = min(tm, tt)
    return pl.pallas_call(
        functools.partial(_mm_res_kernel, precise=precise),
        grid=(tt // tm,),
        in_specs=[pl.BlockSpec((tm, d), lambda i: (i, 0)),
                  pl.BlockSpec((d, d), lambda i: (0, 0)),
                  pl.BlockSpec((tm, d), lambda i: (i, 0))],
        out_specs=pl.BlockSpec((tm, d), lambda i: (i, 0)),
        out_shape=jax.ShapeDtypeStruct((tt, d), F32),
        compiler_params=_cp(("parallel",)),
        name="matmul_residual",
    )(o, w, x)


def _ffn_kernel(x_ref, g_ref, wg_ref, wu_ref, wd_ref, out_ref, xn_ref, acc_ref, *, precise):
    f = pl.program_id(1)

    @pl.when(f == 0)
    def _():
        x = x_ref[...]
        xn_ref[...] = _rms(x, g_ref[...]).astype(xn_ref.dtype)
        acc_ref[...] = x

    xn = xn_ref[...]
    a = _silu(_mm(xn, wg_ref[...], precise)) * _mm(xn, wu_ref[...], precise)
    acc_ref[...] += _mm(a, wd_ref[...], precise)

    @pl.when(f == pl.num_programs(1) - 1)
    def _():
        out_ref[...] = acc_ref[...]


def dense_ffn(x, g, wg, wu, wd, tm=FFN_TILE, tf=FFN_CHUNK, precise=False):
    tt, d = x.shape
    tm = min(tm, tt)
    dff = wg.shape[1]
    return pl.pallas_call(
        functools.partial(_ffn_kernel, precise=precise),
        grid=(tt // tm, dff // tf),
        in_specs=[pl.BlockSpec((tm, d), lambda i, f: (i, 0)),
                  pl.BlockSpec((1, d), lambda i, f: (0, 0)),
                  pl.BlockSpec((d, tf), lambda i, f: (0, f)),
                  pl.BlockSpec((d, tf), lambda i, f: (0, f)),
                  pl.BlockSpec((tf, d), lambda i, f: (f, 0))],
        out_specs=pl.BlockSpec((tm, d), lambda i, f: (i, 0)),
        out_shape=jax.ShapeDtypeStruct((tt, d), F32),
        scratch_shapes=[pltpu.VMEM((tm, d), F32 if precise else BF16), pltpu.VMEM((tm, d), F32)],
        compiler_params=_cp(("parallel", "arbitrary")),
        name="dense_ffn",
    )(x, g.reshape(1, d), wg, wu, wd)


def _router_kernel(x_ref, g_ref, wr_ref, hn_ref, slot_ref, gate_ref, cnt_ref, *, tm):
    xn = _rms(x_ref[...], g_ref[...])
    hn_ref[...] = xn.astype(BF16)
    xh, xl = _split_bf16(xn)
    wh, wl = _split_bf16(wr_ref[...])
    logits = _dot(xh, wh) + _dot(xh, wl) + _dot(xl, wh)
    lane = lax.broadcasted_iota(jnp.int32, (tm, LANES), 1).astype(F32)
    logits = jnp.where(lane < N_EXPERTS, logits, NEG)
    m1 = jnp.max(logits, axis=1, keepdims=True)
    i1 = jnp.min(jnp.where(logits == m1, lane, float(LANES)), axis=1, keepdims=True)
    sel1 = lane == i1
    rest = jnp.where(sel1, NEG, logits)
    m2 = jnp.max(rest, axis=1, keepdims=True)
    i2 = jnp.min(jnp.where(rest == m2, lane, float(LANES)), axis=1, keepdims=True)
    sel2 = lane == i2
    e = jnp.exp(m2 - m1)
    gate_ref[...] = jnp.where(sel1, 1.0 / (1.0 + e), 0.0) + jnp.where(sel2, e / (1.0 + e), 0.0)
    sel = jnp.where(sel1 | sel2, 1.0, 0.0)
    sub = ATT_BLK
    r = lax.broadcasted_iota(jnp.int32, (sub, sub), 0)
    c = lax.broadcasted_iota(jnp.int32, (sub, sub), 1)
    before = jnp.where(r > c, 1.0, 0.0).astype(BF16)
    count = jnp.zeros((1, LANES), F32)
    for b in range(tm // sub):
        sel_b = sel[b * sub:(b + 1) * sub]
        rank = _dot(before, sel_b.astype(BF16)) + count
        slot_ref[b * sub:(b + 1) * sub, :] = jnp.where(sel_b > 0.5, rank, -1.0)
        count = count + jnp.sum(sel_b, axis=0, keepdims=True)
    cnt_ref[...] = jnp.broadcast_to(count, (SUBLANES, LANES))


def router(x, g, w_router, tm):
    tt, d = x.shape
    nt = tt // tm
    wr = jnp.zeros((d, LANES), F32).at[:, :N_EXPERTS].set(w_router)
    return pl.pallas_call(
        functools.partial(_router_kernel, tm=tm),
        grid=(nt,),
        in_specs=[pl.BlockSpec((tm, d), lambda i: (i, 0)),
                  pl.BlockSpec((1, d), lambda i: (0, 0)),
                  pl.BlockSpec((d, LANES), lambda i: (0, 0))],
        out_specs=[pl.BlockSpec((tm, d), lambda i: (i, 0)),
                   pl.BlockSpec((tm, LANES), lambda i: (i, 0)),
                   pl.BlockSpec((tm, LANES), lambda i: (i, 0)),
                   pl.BlockSpec((None, SUBLANES, LANES), lambda i: (i, 0, 0))],
        out_shape=[jax.ShapeDtypeStruct((tt, d), BF16),
                   jax.ShapeDtypeStruct((tt, LANES), F32),
                   jax.ShapeDtypeStruct((tt, LANES), F32),
                   jax.ShapeDtypeStruct((nt, SUBLANES, LANES), F32)],
        compiler_params=_cp(("parallel",)),
        name="router",
    )(x, g.reshape(1, d), wr)


def _moe_kernel(cnt_ref, hn_ref, slotc_ref, slotr_ref, gater_ref, wg_ref, wu_ref, wd_ref,
                out_ref, xc_ref, yacc_ref, *, tm, rows):
    i, e, f = pl.program_id(0), pl.program_id(1), pl.program_id(2)
    nf = pl.num_programs(2)
    nb = (cnt_ref[i * N_EXPERTS + e] + rows - 1) // rows
    slot_r = slotr_ref[...]

    def one_hot_rows(rb):
        rid = (lax.broadcasted_iota(jnp.int32, (rows, tm), 0) + rb * rows).astype(F32)
        return slot_r == rid

    @pl.when((e == 0) & (f == 0))
    def _():
        out_ref[...] = jnp.zeros_like(out_ref)

    @pl.when(f == 0)
    def _():
        def body(rb, carry):
            p = jnp.where(one_hot_rows(rb), 1.0, 0.0).astype(BF16)
            r0 = pl.multiple_of(rb * rows, rows)
            xc_ref[pl.ds(r0, rows), :] = _dot(p, hn_ref[...]).astype(BF16)
            return carry
        lax.fori_loop(0, nb, body, 0)

    def ffn_body(rb, carry):
        r0 = pl.multiple_of(rb * rows, rows)
        xs = xc_ref[pl.ds(r0, rows), :]
        a = (_silu(_dot(xs, wg_ref[...])) * _dot(xs, wu_ref[...])).astype(BF16)
        y = _dot(a, wd_ref[...])

        @pl.when(f == 0)
        def _():
            yacc_ref[pl.ds(r0, rows), :] = y

        @pl.when(f > 0)
        def _():
            yacc_ref[pl.ds(r0, rows), :] += y
        return carry
    lax.fori_loop(0, nb, ffn_body, 0)

    @pl.when(f == nf - 1)
    def _():
        lane = lax.broadcasted_iota(jnp.int32, (tm, LANES), 1)
        slot_c = jnp.sum(jnp.where(lane == e, slotc_ref[...], 0.0), axis=1, keepdims=True)
        gate_r = gater_ref[...]

        def body(rb, carry):
            r0 = pl.multiple_of(rb * rows, rows)
            gc = jnp.sum(jnp.where(one_hot_rows(rb), gate_r, 0.0), axis=1, keepdims=True)
            ys = (yacc_ref[pl.ds(r0, rows), :] * gc).astype(BF16)
            cid = (lax.broadcasted_iota(jnp.int32, (tm, rows), 1) + rb * rows).astype(F32)
            pt = jnp.where(slot_c == cid, 1.0, 0.0).astype(BF16)
            out_ref[...] += _dot(pt, ys)
            return carry
        lax.fori_loop(0, nb, body, 0)


def moe_ffn(x, g, w_router, wg, wu, wd, tm=MOE_TILE, tf=MOE_CHUNK, rows=MOE_ROWS):
    tt, d = x.shape
    nt = tt // tm
    dff = wg.shape[2]
    hn, slot, gate, cnt = router(x, g, w_router, tm)
    cnt_i = cnt[:, 0, :N_EXPERTS].astype(jnp.int32).reshape(nt * N_EXPERTS)

    def to_rows(a):
        a = a[:, :N_EXPERTS].reshape(nt, tm, N_EXPERTS)
        return a.transpose(0, 2, 1).reshape(nt, N_EXPERTS, 1, tm)

    grid_spec = pltpu.PrefetchScalarGridSpec(
        num_scalar_prefetch=1,
        grid=(nt, N_EXPERTS, dff // tf),
        in_specs=[pl.BlockSpec((tm, d), lambda i, e, f, c: (i, 0)),
                  pl.BlockSpec((tm, LANES), lambda i, e, f, c: (i, 0)),
                  pl.BlockSpec((None, None, 1, tm), lambda i, e, f, c: (i, e, 0, 0)),
                  pl.BlockSpec((None, None, 1, tm), lambda i, e, f, c: (i, e, 0, 0)),
                  pl.BlockSpec((None, d, tf), lambda i, e, f, c: (e, 0, f)),
                  pl.BlockSpec((None, d, tf), lambda i, e, f, c: (e, 0, f)),
                  pl.BlockSpec((None, tf, d), lambda i, e, f, c: (e, f, 0))],
        out_specs=pl.BlockSpec((tm, d), lambda i, e, f, c: (i, 0)),
        scratch_shapes=[pltpu.VMEM((_round_up(tm, rows), d), BF16), pltpu.VMEM((_round_up(tm, rows), d), F32)],
    )
    return pl.pallas_call(
        functools.partial(_moe_kernel, tm=tm, rows=rows),
        grid_spec=grid_spec,
        out_shape=jax.ShapeDtypeStruct((tt, d), F32),
        compiler_params=_cp(("parallel", "arbitrary", "arbitrary"), MOE_VMEM_LIMIT),
        name="moe_ffn",
    )(cnt_i, hn, slot, to_rows(slot), to_rows(gate), wg, wu, wd)


def _after_matrix(n):
    r = lax.broadcasted_iota(jnp.int32, (n, n), 0)
    c = lax.broadcasted_iota(jnp.int32, (n, n), 1)
    return jnp.where(r > c, 1.0, 0.0).astype(BF16)


def _sb_logs(z, valid):
    t = jnp.log(1.0 + jnp.exp2(jnp.abs(z) * -LOG2E))
    log_beta = jnp.minimum(z, 0.0) - t
    log_keep = log_beta - z
    if valid is not None:
        log_keep = jnp.where(valid, log_keep, 0.0)
    return log_beta, log_keep


def _sb_keep_after(log_keep, after):
    return _dot(jnp.concatenate(_split_bf16(log_keep), axis=1), jnp.concatenate([after, after], axis=0))


def _sb_weights(log_beta, log_keep, keep_after, carry, valid):
    w = jnp.exp(log_beta + keep_after + carry)
    if valid is not None:
        w = jnp.where(valid, w, 0.0)
    return w, carry + keep_after[:, 0:1] + log_keep[:, 0:1]


def _sb_block(z, valid, after, carry):
    log_beta, log_keep = _sb_logs(z, valid)
    return _sb_weights(log_beta, log_keep, _sb_keep_after(log_keep, after), carry, valid)


def _head_pair(q2):
    lane = lax.broadcasted_iota(jnp.int32, q2.shape, 1)
    zero = jnp.zeros_like(q2)
    return jnp.where(lane < DH, q2, zero), jnp.where(lane >= DH, q2, zero)


def _sb_prompt_kernel(o_hbm_ref, q_ref, k_ref, v_ref, o_ref, acc_ref, car_ref, *, blk, n_pair):
    del o_hbm_ref
    qi = pl.program_id(2)
    n_head = 2 * n_pair
    qh = []
    for p in range(n_pair):
        qh += list(_head_pair(q_ref[:, p * LANES:(p + 1) * LANES]))
    after = _after_matrix(blk)
    acc_ref[...] = jnp.zeros_like(acc_ref)
    car_ref[...] = jnp.zeros_like(car_ref)
    row = lax.broadcasted_iota(jnp.int32, (blk, blk), 0)
    col = lax.broadcasted_iota(jnp.int32, (blk, blk), 1)

    def block(kj, valid):
        k0 = pl.multiple_of(kj * blk, blk)
        kk = [k_ref[pl.ds(k0, blk), p * LANES:(p + 1) * LANES] for p in range(n_pair)]
        vv = [v_ref[pl.ds(k0, blk), p * LANES:(p + 1) * LANES] for p in range(n_pair)]
        zs = [_dot_nt(qh[h], kk[h // 2]) for h in range(n_head)]
        logs = [_sb_logs(z, valid) for z in zs]
        keep_after = [_sb_keep_after(lk, after) for _, lk in logs]
        ws = []
        for h in range(n_head):
            w, car_ref[h] = _sb_weights(logs[h][0], logs[h][1], keep_after[h], car_ref[h], valid)
            ws.append(w.astype(BF16))
        for h in range(n_head):
            acc_ref[h] += _dot(ws[h], vv[h // 2])

    block(qi, col < row)

    def body(s, carry):
        block(qi - 1 - s, None)
        return carry
    lax.fori_loop(0, qi, body, 0)
    lane = lax.broadcasted_iota(jnp.int32, (blk, LANES), 1)
    for p in range(n_pair):
        o_ref[:, p * LANES:(p + 1) * LANES] = jnp.where(lane < DH, acc_ref[2 * p], acc_ref[2 * p + 1])


def sb_prompt_attention(o_init, q_bf, k_bf, v_bf, n_batch, lpad, blk=ATT_BLK, n_pair=ATT_PAIRS):
    tt, d = q_bf.shape
    nq = lpad // blk
    wid = n_pair * LANES
    return pl.pallas_call(
        functools.partial(_sb_prompt_kernel, blk=blk, n_pair=n_pair),
        grid=(n_batch, d // wid, nq),
        in_specs=[pl.BlockSpec(memory_space=pl.ANY),
                  pl.BlockSpec((blk, wid), lambda b, h, i: (b * nq + i, h)),
                  pl.BlockSpec((lpad, wid), lambda b, h, i: (b, h)),
                  pl.BlockSpec((lpad, wid), lambda b, h, i: (b, h))],
        out_specs=pl.BlockSpec((blk, wid), lambda b, h, i: (b * nq + i, h)),
        out_shape=jax.ShapeDtypeStruct((tt, d), F32),
        scratch_shapes=[pltpu.VMEM((2 * n_pair, blk, LANES), F32), pltpu.VMEM((2 * n_pair, blk, 1), F32)],
        input_output_aliases={0: 0},
        compiler_params=_cp(("parallel", "parallel", "arbitrary")),
        name="sb_prompt_attention",
    )(o_init, q_bf, k_bf, v_bf)


def _lambda(lam_ref, lam_init):
    a = jnp.sum(lam_ref[0:1, :] * lam_ref[1:2, :], axis=1, keepdims=True)
    b = jnp.sum(lam_ref[2:3, :] * lam_ref[3:4, :], axis=1, keepdims=True)
    return jnp.exp(a) - jnp.exp(b) + lam_init


def _diff_prompt_kernel(o_hbm_ref, lam_ref, g_ref, q_ref, k_ref, v_ref, o_ref, m_ref, l_ref, acc_ref,
                        *, blk, n_pair, lam_init):
    del o_hbm_ref
    qi = pl.program_id(2)
    n_map = 2 * n_pair
    qh = []
    for p in range(n_pair):
        qh += list(_head_pair(q_ref[:, p * LANES:(p + 1) * LANES]))
    m_ref[...] = jnp.full_like(m_ref, NEG)
    l_ref[...] = jnp.zeros_like(l_ref)
    acc_ref[...] = jnp.zeros_like(acc_ref)
    row = lax.broadcasted_iota(jnp.int32, (blk, blk), 0)
    col = lax.broadcasted_iota(jnp.int32, (blk, blk), 1)
    ones = jnp.ones((blk, LANES), BF16)

    def block(kj, valid):
        k0 = pl.multiple_of(kj * blk, blk)
        kk = [k_ref[pl.ds(k0, blk), p * LANES:(p + 1) * LANES] for p in range(n_pair)]
        vv = [v_ref[pl.ds(k0, blk), p * LANES:(p + 1) * LANES] for p in range(n_pair)]
        ss = [_dot_nt(qh[h], kk[h // 2]) for h in range(n_map)]
        if valid is not None:
            ss = [jnp.where(valid, s, NEG) for s in ss]
        ps, alphas = [], []
        for h in range(n_map):
            m_old = m_ref[h]
            m_new = jnp.maximum(m_old, jnp.max(ss[h], axis=1, keepdims=True))
            alphas.append(jnp.exp(m_old - m_new))
            ps.append(jnp.exp(ss[h] - jnp.concatenate([m_new] * (blk // LANES), axis=1)).astype(BF16))
            m_ref[h] = m_new
        v1 = [jnp.concatenate([v, ones], axis=1) for v in vv]
        for h in range(n_map):
            pv = _dot(ps[h], v1[h // 2])
            acc_ref[h] = alphas[h] * acc_ref[h] + pv[:, :LANES]
            l_ref[h] = alphas[h] * l_ref[h] + pv[:, LANES:]

    block(qi, col <= row)

    def body(s, carry):
        block(s, None)
        return carry
    lax.fori_loop(0, qi, body, 0)
    lam = _lambda(lam_ref, lam_init)
    for p in range(n_pair):
        o = acc_ref[2 * p] / l_ref[2 * p] - lam * (acc_ref[2 * p + 1] / l_ref[2 * p + 1])
        o_ref[:, p * LANES:(p + 1) * LANES] = _rms(o, g_ref[...]) * (1.0 - lam_init)


def diff_prompt_attention(o_init, lam4, subln, q_bf, k_bf, v_bf, n_batch, lpad, lam_init, blk=ATT_BLK,
                          n_pair=ATT_PAIRS):
    tt, d = q_bf.shape
    nq = lpad // blk
    wid = n_pair * LANES
    return pl.pallas_call(
        functools.partial(_diff_prompt_kernel, blk=blk, n_pair=n_pair, lam_init=lam_init),
        grid=(n_batch, d // wid, nq),
        in_specs=[pl.BlockSpec(memory_space=pl.ANY),
                  pl.BlockSpec((4, DH), lambda b, h, i: (0, 0)),
                  pl.BlockSpec((1, LANES), lambda b, h, i: (0, 0)),
                  pl.BlockSpec((blk, wid), lambda b, h, i: (b * nq + i, h)),
                  pl.BlockSpec((lpad, wid), lambda b, h, i: (b, h)),
                  pl.BlockSpec((lpad, wid), lambda b, h, i: (b, h))],
        out_specs=pl.BlockSpec((blk, wid), lambda b, h, i: (b * nq + i, h)),
        out_shape=jax.ShapeDtypeStruct((tt, d), F32),
        scratch_shapes=[pltpu.VMEM((2 * n_pair, blk, LANES), F32)] * 3,
        input_output_aliases={0: 0},
        compiler_params=_cp(("parallel", "parallel", "arbitrary")),
        name="diff_prompt_attention",
    )(o_init, lam4, subln.reshape(1, LANES), q_bf, k_bf, v_bf)


def _block_diag_mask(n_heads_rows, lane_group):
    shape = (n_heads_rows * SUBLANES, D_MODEL)
    rh = lax.broadcasted_iota(jnp.int32, shape, 0) // SUBLANES
    lh = lax.broadcasted_iota(jnp.int32, shape, 1) // lane_group
    return rh, lh


def _page_tokens(ref):
    n_heads = ref.shape[0] // LANES
    heads = [ref[pl.ds(h, LANES, stride=n_heads), :] for h in range(n_heads)]
    return jnp.concatenate(heads, axis=1).astype(BF16)


def _page_values(w, v_ref, v_by_width, precise=False):
    if v_by_width:
        return _mm(w, v_ref[...], precise, nt=True)
    assert not precise
    return _dot(w.astype(BF16), _page_tokens(v_ref))


def _new_token_masks():
    key = lax.broadcasted_iota(jnp.int32, (LANES, LANES), 1)
    tq = lax.broadcasted_iota(jnp.int32, (LANES, LANES), 0) % SUBLANES
    return key, tq


def _sample_setup(q_ref, kn_ref, vn_ref, qbd_ref, kpad_ref, vpad_ref):
    q = q_ref[...] * (DH ** -0.5)
    qt = jnp.broadcast_to(q[None], (H_SB, SUBLANES, D_MODEL)).reshape(H_SB * SUBLANES, D_MODEL)
    rh, lh = _block_diag_mask(H_SB, DH)
    qbd_ref[...] = jnp.where(rh == lh, qt, 0.0).astype(qbd_ref.dtype)
    kpad_ref[...] = jnp.zeros_like(kpad_ref)
    vpad_ref[...] = jnp.zeros_like(vpad_ref)
    kpad_ref[0:SUBLANES, :] = kn_ref[...]
    vpad_ref[0:SUBLANES, :] = vn_ref[...]


def _sb_sample_kernel(pt_ref, o_hbm_ref, q_ref, kn_ref, vn_ref, *rest, n_pg, v_by_width, precise):
    del pt_ref, o_hbm_ref
    k_refs, v_refs = rest[:n_pg], rest[n_pg:2 * n_pg]
    o_ref, qbd_ref, acc_ref, car_ref, kpad_ref, vpad_ref = rest[2 * n_pg:]
    st = pl.program_id(1)
    after = _after_matrix(LANES)

    @pl.when(st == 0)
    def _():
        _sample_setup(q_ref, kn_ref, vn_ref, qbd_ref, kpad_ref, vpad_ref)
        key, tq = _new_token_masks()
        z = _mm(qbd_ref[...], kpad_ref[...], precise, nt=True)
        w, car_ref[...] = _sb_block(z, key < tq, after, jnp.zeros(car_ref.shape, F32))
        acc_ref[...] = _mm(w, vpad_ref[...], precise)

    @pl.when(st > 0)
    def _():
        order = list(reversed(range(n_pg)))
        qbd = qbd_ref[...]
        zs = [_mm(qbd, k_refs[p][...], precise) for p in order]
        logs = [_sb_logs(z, None) for z in zs]
        keep_after = [_sb_keep_after(lk, after) for _, lk in logs]
        car = car_ref[...]
        ws = []
        for i in range(n_pg):
            w, car = _sb_weights(logs[i][0], logs[i][1], keep_after[i], car, None)
            ws.append(w)
        car_ref[...] = car
        acc = acc_ref[...]
        for i, p in enumerate(order):
            acc = acc + _page_values(ws[i], v_refs[p], v_by_width, precise)
        acc_ref[...] = acc

    @pl.when(st == pl.num_programs(1) - 1)
    def _():
        rh, lh = _block_diag_mask(H_SB, DH)
        a = jnp.where(rh == lh, acc_ref[...], 0.0).reshape(H_SB, SUBLANES, D_MODEL)
        o_ref[...] = jnp.sum(a, axis=0)


def _diff_sample_kernel(pt_ref, o_hbm_ref, lam_ref, g_ref, q_ref, kn_ref, vn_ref, *rest, n_pg, v_by_width,
                        lam_init):
    del pt_ref, o_hbm_ref
    k_refs, v_refs = rest[:n_pg], rest[n_pg:2 * n_pg]
    o_ref, qbd_ref, acc_ref, m_ref, l_ref, kpad_ref, vpad_ref = rest[2 * n_pg:]
    st = pl.program_id(1)

    @pl.when(st == 0)
    def _():
        _sample_setup(q_ref, kn_ref, vn_ref, qbd_ref, kpad_ref, vpad_ref)
        key, tq = _new_token_masks()
        s = jnp.where(key <= tq, _dot_nt(qbd_ref[...], kpad_ref[...].astype(BF16)), NEG)
        m = jnp.max(s, axis=1, keepdims=True)
        p = jnp.exp(s - m)
        m_ref[...] = m
        l_ref[...] = jnp.sum(p, axis=1, keepdims=True)
        acc_ref[...] = _dot(p.astype(BF16), vpad_ref[...].astype(BF16))

    @pl.when(st > 0)
    def _():
        qbd = qbd_ref[...]
        ss = [_dot(qbd, k_refs[p][...].astype(BF16)) for p in range(n_pg)]
        top = ss[0]
        for s in ss[1:]:
            top = jnp.maximum(top, s)
        m_old = m_ref[...]
        m_new = jnp.maximum(m_old, jnp.max(top, axis=1, keepdims=True))
        alpha = jnp.exp(m_old - m_new)
        ps = [jnp.exp(s - m_new) for s in ss]
        tot = ps[0]
        for p in ps[1:]:
            tot = tot + p
        l_ref[...] = alpha * l_ref[...] + jnp.sum(tot, axis=1, keepdims=True)
        m_ref[...] = m_new
        acc = alpha * acc_ref[...]
        for i in range(n_pg):
            acc = acc + _page_values(ps[i].astype(BF16), v_refs[i], v_by_width)
        acc_ref[...] = acc

    @pl.when(st == pl.num_programs(1) - 1)
    def _():
        lam = _lambda(lam_ref, lam_init)
        rh, lh = _block_diag_mask(2 * H_DIFF, 2 * DH)
        coef = jnp.where(rh % 2 == 0, 1.0, -lam)
        a = jnp.where(rh // 2 == lh, acc_ref[...] / l_ref[...] * coef, 0.0)
        o = jnp.sum(a.reshape(2 * H_DIFF, SUBLANES, D_MODEL), axis=0)
        for h in range(H_DIFF):
            sl = slice(h * 2 * DH, (h + 1) * 2 * DH)
            o_ref[:, sl] = _rms(o[:, sl], g_ref[...]) * (1.0 - lam_init)


def sample_attention(kind, o_init, page_table, q, k_new, v_new, cache_k, cache_v, layer, row0, n_seq,
                     lam4=None, subln=None, lam_init=None, n_pg=PAGES_PER_STEP, q_row0=None, precise=False):
    tt, d = o_init.shape
    n_pages = page_table.shape[1]
    page = cache_k.shape[2]
    assert page == LANES and n_pages % n_pg == 0 and row0 % SUBLANES == 0
    n_grp = n_pages // n_pg
    n_pool = cache_k.shape[1]
    def page_view(c):
        n_layers, _, _, heads, width = c.shape
        if width < LANES:
            return jnp.transpose(c, (0, 1, 3, 4, 2)).reshape(n_layers, n_pool, heads * width, page), True
        return c.reshape(n_layers, n_pool, page * heads, width), False

    ck, k_by_width = page_view(cache_k)
    cv, v_by_width = page_view(cache_v)
    assert k_by_width and ck.shape[2:] == cv.shape[2:] == (d, LANES)
    blk0 = row0 // SUBLANES
    blk0_q = blk0 if q_row0 is None else q_row0 // SUBLANES

    def tok_spec(first=None):
        first = blk0_q if first is None else first
        return pl.BlockSpec((SUBLANES, d), lambda s, st, pt: (first + s, 0))

    def page_spec(p):
        if kind == "sb":
            def im(s, st, pt):
                grp = jnp.minimum(n_grp - st, n_grp - 1)
                return (layer, pt[s, grp * n_pg + p], 0, 0)
        else:
            def im(s, st, pt):
                grp = jnp.maximum(st - 1, 0)
                return (layer, pt[s, grp * n_pg + p], 0, 0)
        return pl.BlockSpec((None, None, d, LANES), im)

    rows = H_SB * SUBLANES
    common_scratch = [pltpu.VMEM((rows, d), F32 if precise else BF16), pltpu.VMEM((rows, d), F32)]
    pad_scratch = [pltpu.VMEM((LANES, d), F32), pltpu.VMEM((LANES, d), F32)]
    page_specs = [page_spec(p) for p in range(n_pg)] * 2
    page_args = [ck] * n_pg + [cv] * n_pg
    if kind == "sb":
        body = functools.partial(_sb_sample_kernel, n_pg=n_pg, v_by_width=v_by_width, precise=precise)
        extra_specs, extra_args = [], []
        scratch = common_scratch + [pltpu.VMEM((rows, 1), F32)] + pad_scratch
    else:
        body = functools.partial(_diff_sample_kernel, n_pg=n_pg, v_by_width=v_by_width, lam_init=lam_init)
        extra_specs = [pl.BlockSpec((4, DH), lambda s, st, pt: (0, 0)),
                       pl.BlockSpec((1, LANES), lambda s, st, pt: (0, 0))]
        extra_args = [lam4, subln.reshape(1, LANES)]
        scratch = common_scratch + [pltpu.VMEM((rows, 1), F32), pltpu.VMEM((rows, 1), F32)] + pad_scratch
    grid_spec = pltpu.PrefetchScalarGridSpec(
        num_scalar_prefetch=1,
        grid=(n_seq, n_grp + 1),
        in_specs=[pl.BlockSpec(memory_space=pl.ANY)] + extra_specs
                 + [tok_spec(), tok_spec(), tok_spec()] + page_specs,
        out_specs=tok_spec(blk0),
        scratch_shapes=scratch,
    )
    return pl.pallas_call(
        body,
        grid_spec=grid_spec,
        out_shape=jax.ShapeDtypeStruct((tt, d), F32),
        input_output_aliases={1: 0},
        compiler_params=_cp(("parallel", "arbitrary")),
        name=kind + "_sample_attention",
    )(page_table, o_init, *extra_args, q, k_new, v_new, *page_args)


def _conv_kernel(gb_ref, gc_ref, hv_ref, hgc_ref, hhv_ref, p1_ref, p2_ref, w_ref, o_ref, u_ref,
                 *, tiles_per_seq, sample_tile, dec_seq):
    i = pl.program_id(0)
    u = gc_ref[...] * hv_ref[...]
    u_ref[...] = u
    row = lax.broadcasted_iota(jnp.int32, u.shape, 0)
    r1 = pltpu.roll(u, 1, axis=0)
    r2 = pltpu.roll(u, 2, axis=0)
    w0, w1, w2 = w_ref[0:1, :], w_ref[1:2, :], w_ref[2:3, :]

    def finish(prev1, prev2):
        o_ref[...] = gb_ref[...] * (w0 * prev2 + w1 * prev1 + w2 * u)

    @pl.when(i == sample_tile)
    def _():
        pos = row % dec_seq
        finish(jnp.where(pos == 0, p1_ref[...], r1), jnp.where(pos < 2, p2_ref[...], r2))

    @pl.when(i != sample_tile)
    def _():
        hu = hgc_ref[...] * hhv_ref[...]
        hu = jnp.where(i % tiles_per_seq == 0, 0.0, hu)
        h7, h6 = hu[7:8, :], hu[6:7, :]
        finish(jnp.where(row == 0, h7, r1), jnp.where(row == 0, h6, jnp.where(row == 1, h7, r2)))


def gated_conv(gb, gc, hv, prev1, prev2, w, lpad, sample_row0, dec_seq, tc=ATT_BLK):
    tt, d = gb.shape
    assert lpad % tc == 0 and sample_row0 % tc == 0 and prev1.shape[0] == tc
    per = tc // SUBLANES

    def tile():
        return pl.BlockSpec((tc, d), lambda i: (i, 0))

    def halo():
        return pl.BlockSpec((SUBLANES, d), lambda i: (jnp.maximum(i * per - 1, 0), 0))

    def whole(r):
        return pl.BlockSpec((r, d), lambda i: (0, 0))

    return pl.pallas_call(
        functools.partial(_conv_kernel, tiles_per_seq=lpad // tc, sample_tile=sample_row0 // tc,
                          dec_seq=dec_seq),
        grid=(tt // tc,),
        in_specs=[tile(), tile(), tile(), halo(), halo(), whole(tc), whole(tc), whole(w.shape[0])],
        out_specs=[tile(), tile()],
        out_shape=[jax.ShapeDtypeStruct((tt, d), F32), jax.ShapeDtypeStruct((tt, d), F32)],
        compiler_params=_cp(("parallel",)),
        name="gated_conv",
    )(gb, gc, hv, gc, hv, prev1, prev2, w)


def _norm_kernel(x_ref, d_ref, g_ref, o_ref, *, has_delta):
    x = x_ref[...]
    if has_delta:
        x = x + d_ref[...]
    o_ref[...] = _rms(x, g_ref[...])


def final_norm(x, g, delta=None, tm=ROW_TILE):
    tt, d = x.shape
    d_arg, d_spec = _delta_operand(x, delta, tm)
    return pl.pallas_call(
        functools.partial(_norm_kernel, has_delta=delta is not None),
        grid=(tt // tm,),
        in_specs=[pl.BlockSpec((tm, d), lambda i: (i, 0)), d_spec, pl.BlockSpec((1, d), lambda i: (0, 0))],
        out_specs=pl.BlockSpec((tm, d), lambda i: (i, 0)),
        out_shape=jax.ShapeDtypeStruct((tt, d), F32),
        compiler_params=_cp(("parallel",)),
        name="final_norm",
    )(x, d_arg, g.reshape(1, d))


def _round_up(a, b):
    return -(-a // b) * b


def kernel(x_prompt, x_sample, cache_k_sb, cache_v_sb, cache_k_diff, cache_v_diff, state_conv, page_table,
           meta_tokens, norm_mix, w_mix_in, w_mix_out, conv_w, diff_lambda_q1, diff_lambda_k1,
           diff_lambda_q2, diff_lambda_k2, diff_subln, norm_ffn, w_dense_gate, w_dense_up, w_dense_down,
           w_router, w_exp_gate, w_exp_up, w_exp_down, norm_final):
    bp, seq, d = x_prompt.shape
    bs, ts, _ = x_sample.shape
    depth = w_mix_in.shape[0]
    assert d == D_MODEL and ts == SUBLANES and H_SB * ts == LANES
    lp = seq + N_META
    lpad = _round_up(lp, ATT_BLK)
    n_s = bs * ts
    row_s = bp * lpad
    tt = _round_up(row_s + n_s, math.lcm(ROW_TILE, FFN_TILE, MOE_TILE, ATT_BLK))
    assert n_s == ATT_BLK

    meta = jnp.broadcast_to(meta_tokens[None].astype(F32), (bp, N_META, d))
    xp = jnp.concatenate([meta, x_prompt, jnp.zeros((bp, lpad - lp, d), F32)], axis=1)
    x = jnp.concatenate([xp.reshape(row_s, d), x_sample.reshape(n_s, d),
                         jnp.zeros((tt - row_s - n_s, d), F32)], axis=0)

    def prompt_rows(a):
        return a[:row_s].reshape(bp, lpad, -1)[:, :lp]

    def sample_rows(a):
        return a[row_s:row_s + n_s].reshape(bs, ts, -1)

    outs = {name: [] for name in ("sb_kp", "sb_vp", "sb_ks", "sb_vs", "df_kp", "df_vp", "df_ks", "df_vs",
                                  "cv_p", "cv_s")}
    att_plan = [(True, DH ** -0.5), (True, 1.0), (True, 1.0)]
    pending = None
    xs = x_sample.reshape(n_s, d).astype(F32)
    for i in range(depth):
        kind, j = i % N_MIXERS, i // N_MIXERS
        w_in = w_mix_in[i].astype(BF16)
        if kind == 1:
            proj = norm_matmul(x, norm_mix[i], w_in, [(True, None)] * 3, delta=pending)
            if pending is not None:
                x, proj = proj[0], proj[1:]
            gb, gc, hv = proj
            st = state_conv[j].astype(F32)
            zero = jnp.zeros((bs, ts, d), F32)
            prev1 = zero.at[:, 0].set(st[:, 1]).reshape(n_s, d)
            prev2 = zero.at[:, 0].set(st[:, 0]).at[:, 1].set(st[:, 1]).reshape(n_s, d)
            o, u = gated_conv(gb, gc, hv, prev1, prev2, conv_w[j], lpad, row_s, ts)
            if xs is not None:
                proj_s = norm_matmul(xs, norm_mix[i], w_mix_in[i], [(True, None)] * 3, precise=True)
                o_s, _ = gated_conv(*proj_s, prev1, prev2, conv_w[j], ATT_BLK, 0, ts)
            outs["cv_p"].append(prompt_rows(u)[:, -(conv_w.shape[1] - 1):])
            outs["cv_s"].append(sample_rows(u)[:, -(conv_w.shape[1] - 1):])
        else:
            proj = norm_matmul(x, norm_mix[i], w_in, att_plan, delta=pending)
            if pending is not None:
                x, proj = proj[0], proj[1:]
            q, q_bf, k, k_bf, v, v_bf = proj
            o = jnp.zeros((tt, d), F32)
            if kind == 0:
                o = sb_prompt_attention(o, q_bf, k_bf, v_bf, bp, lpad)
                if xs is not None:
                    q_s, k_s, v_s = norm_matmul(xs, norm_mix[i], w_mix_in[i], [(True, None)] * 3, precise=True)
                    o = sample_attention("sb", o, page_table, q_s, k_s, v_s, cache_k_sb, cache_v_sb, j, row_s,
                                         bs, q_row0=0, precise=True)
                    o_s = o[row_s:row_s + n_s]
                else:
                    o = sample_attention("sb", o, page_table, q, k, v, cache_k_sb, cache_v_sb, j, row_s, bs)
                pre, hk, hv_ = "sb", (H_SB, DH), (H_SB, DH)
            else:
                xs = None
                lam_init = 0.8 - 0.6 * math.exp(-0.3 * i)
                lam4 = jnp.stack([diff_lambda_q1[j], diff_lambda_k1[j], diff_lambda_q2[j],
                                  diff_lambda_k2[j]]).astype(F32)
                o = diff_prompt_attention(o, lam4, diff_subln[j], q_bf, k_bf, v_bf, bp, lpad, lam_init)
                o = sample_attention("diff", o, page_table, q, k, v, cache_k_diff, cache_v_diff, j, row_s,
                                     bs, lam4=lam4, subln=diff_subln[j], lam_init=lam_init)
                pre, hk, hv_ = "df", (2 * H_DIFF, DH), (H_DIFF, 2 * DH)
            outs[pre + "_kp"].append(prompt_rows(k).reshape(bp, lp, *hk))
            outs[pre + "_vp"].append(prompt_rows(v).reshape(bp, lp, *hv_))
            outs[pre + "_ks"].append(sample_rows(k).reshape(bs, ts, *hk))
            outs[pre + "_vs"].append(sample_rows(v).reshape(bs, ts, *hv_))
        x = matmul_residual(o, w_mix_out[i].astype(BF16), x)
        if xs is not None:
            xs = matmul_residual(o_s, w_mix_out[i], xs, precise=True)

        m = i // 2
        if i % 2 == 0:
            x = dense_ffn(x, norm_ffn[i], w_dense_gate[m].astype(BF16), w_dense_up[m].astype(BF16),
                          w_dense_down[m].astype(BF16))
            if xs is not None:
                xs = dense_ffn(xs, norm_ffn[i], w_dense_gate[m], w_dense_up[m], w_dense_down[m], precise=True)
            pending = None
        else:
            if xs is not None:
                x = lax.dynamic_update_slice(x, xs, (row_s, 0))
                xs = None
            pending = moe_ffn(x, norm_ffn[i], w_router[m], w_exp_gate[m].astype(BF16),
                              w_exp_up[m].astype(BF16), w_exp_down[m].astype(BF16))

    y = final_norm(x, norm_final, delta=pending)
    y_prompt = prompt_rows(y)[:, N_META:]
    y_sample = sample_rows(y)
    return (y_prompt, y_sample,
            jnp.stack(outs["sb_kp"]), jnp.stack(outs["sb_vp"]), jnp.stack(outs["sb_ks"]),
            jnp.stack(outs["sb_vs"]), jnp.stack(outs["df_kp"]), jnp.stack(outs["df_vp"]),
            jnp.stack(outs["df_ks"]), jnp.stack(outs["df_vs"]),
            jnp.stack(outs["cv_p"]), jnp.stack(outs["cv_s"]))
```

```python
import functools
import math

import jax
import jax.numpy as jnp
from jax import lax
from jax.experimental import pallas as pl
from jax.experimental.pallas import tpu as pltpu

F32 = jnp.float32
BF16 = jnp.bfloat16

D_MODEL = 1024
N_META = 16
N_MIXERS = 3
H_SB = 16
DH = 64
H_DIFF = 8
N_EXPERTS = 8
EPS = 1e-6
LANES = 128
SUBLANES = 8
VMEM_LIMIT = 56 * 1024 * 1024

ATT_BLK = 256
ROW_TILE = 512
FFN_TILE = 896
FFN_CHUNK = 512
MOE_TILE = 1792
MOE_CHUNK = 896
MOE_ROWS = 512
MOE_VMEM_LIMIT = 60 * 1024 * 1024
PAGES_PER_STEP = 8
ATT_PAIRS = 4
NEG = -1e30
LOG2E = 1.4426950408889634


def _cp(sem, vmem=VMEM_LIMIT):
    return pltpu.CompilerParams(dimension_semantics=sem, vmem_limit_bytes=vmem)


def _rms(x, g):
    ms = jnp.mean(x * x, axis=-1, keepdims=True)
    return x * lax.rsqrt(ms + EPS) * g


def _dot(a, b):
    return jnp.dot(a, b, preferred_element_type=F32)


def _dot_nt(a, b):
    return lax.dot_general(a, b, (((1,), (1,)), ((), ())), preferred_element_type=F32)


def _split_bf16(x):
    hi = x.astype(BF16)
    lo = (x - hi.astype(F32)).astype(BF16)
    return hi, lo


def _mm(a, b, precise, nt=False):
    dot = _dot_nt if nt else _dot
    if not precise:
        return dot(a.astype(BF16), b.astype(BF16))
    ah, al = _split_bf16(a)
    bh, bl = _split_bf16(b)
    return dot(ah, bh) + dot(ah, bl) + dot(al, bh)


def _silu(g):
    return g / (1.0 + jnp.exp(-g))


def _norm_mm_kernel(x_ref, d_ref, g_ref, w_ref, *out_refs, plan, has_delta, precise):
    x = x_ref[...]
    oi = 0
    if has_delta:
        x = x + d_ref[...]
        out_refs[0][...] = x
        oi = 1
    xn = _rms(x, g_ref[...])
    if not precise:
        xn = xn.astype(BF16)
    for c, (want_f32, bf_scale) in enumerate(plan):
        y = _mm(xn, w_ref[:, c * D_MODEL:(c + 1) * D_MODEL], precise)
        if want_f32:
            out_refs[oi][...] = y
            oi += 1
        if bf_scale is not None:
            out_refs[oi][...] = (y * bf_scale).astype(BF16)
            oi += 1


def _delta_operand(x, delta, tm):
    d = x.shape[1]
    if delta is None:
        return x, pl.BlockSpec((SUBLANES, d), lambda i: (0, 0))
    return delta, pl.BlockSpec((tm, d), lambda i: (i, 0))


def norm_matmul(x, g, w, plan, delta=None, tm=ROW_TILE, precise=False):
    tt, d = x.shape
    tm = min(tm, tt)
    n = w.shape[1]
    dts = [F32] if delta is not None else []
    for want_f32, bf_scale in plan:
        dts += ([F32] if want_f32 else []) + ([BF16] if bf_scale is not None else [])
    d_arg, d_spec = _delta_operand(x, delta, tm)
    return pl.pallas_call(
        functools.partial(_norm_mm_kernel, plan=tuple(plan), has_delta=delta is not None, precise=precise),
        grid=(tt // tm,),
        in_specs=[pl.BlockSpec((tm, d), lambda i: (i, 0)),
                  d_spec,
                  pl.BlockSpec((1, d), lambda i: (0, 0)),
                  pl.BlockSpec((d, n), lambda i: (0, 0))],
        out_specs=[pl.BlockSpec((tm, d), lambda i: (i, 0)) for _ in dts],
        out_shape=[jax.ShapeDtypeStruct((tt, d), dt) for dt in dts],
        compiler_params=_cp(("parallel",)),
        name="norm_matmul",
    )(x, d_arg, g.reshape(1, d), w)


def _mm_res_kernel(o_ref, w_ref, x_ref, out_ref, *, precise):
    out_ref[...] = x_ref[...] + _mm(o_ref[...], w_ref[...], precise)


def matmul_residual(o, w, x, tm=ROW_TILE, precise=False):
    tt, d = x.shape
    tm = min(tm, tt)
    return pl.pallas_call(
        functools.partial(_mm_res_kernel, precise=precise),
        grid=(tt // tm,),
        in_specs=[pl.BlockSpec((tm, d), lambda i: (i, 0)),
                  pl.BlockSpec((d, d), lambda i: (0, 0)),
                  pl.BlockSpec((tm, d), lambda i: (i, 0))],
        out_specs=pl.BlockSpec((tm, d), lambda i: (i, 0)),
        out_shape=jax.ShapeDtypeStruct((tt, d), F32),
        compiler_params=_cp(("parallel",)),
        name="matmul_residual",
    )(o, w, x)


def _ffn_kernel(x_ref, g_ref, wg_ref, wu_ref, wd_ref, out_ref, xn_ref, acc_ref, *, precise):
    f = pl.program_id(1)

    @pl.when(f == 0)
    def _():
        x = x_ref[...]
        xn_ref[...] = _rms(x, g_ref[...]).astype(xn_ref.dtype)
        acc_ref[...] = x

    xn = xn_ref[...]
    a = _silu(_mm(xn, wg_ref[...], precise)) * _mm(xn, wu_ref[...], precise)
    acc_ref[...] += _mm(a, wd_ref[...], precise)

    @pl.when(f == pl.num_programs(1) - 1)
    def _():
        out_ref[...] = acc_ref[...]


def dense_ffn(x, g, wg, wu, wd, tm=FFN_TILE, tf=FFN_CHUNK, precise=False):
    tt, d = x.shape
    tm = min(tm, tt)
    dff = wg.shape[1]
    return pl.pallas_call(
        functools.partial(_ffn_kernel, precise=precise),
        grid=(tt // tm, dff // tf),
        in_specs=[pl.BlockSpec((tm, d), lambda i, f: (i, 0)),
                  pl.BlockSpec((1, d), lambda i, f: (0, 0)),
                  pl.BlockSpec((d, tf), lambda i, f: (0, f)),
                  pl.BlockSpec((d, tf), lambda i, f: (0, f)),
                  pl.BlockSpec((tf, d), lambda i, f: (f, 0))],
        out_specs=pl.BlockSpec((tm, d), lambda i, f: (i, 0)),
        out_shape=jax.ShapeDtypeStruct((tt, d), F32),
        scratch_shapes=[pltpu.VMEM((tm, d), F32 if precise else BF16), pltpu.VMEM((tm, d), F32)],
        compiler_params=_cp(("parallel", "arbitrary")),
        name="dense_ffn",
    )(x, g.reshape(1, d), wg, wu, wd)


def _router_kernel(x_ref, g_ref, wr_ref, hn_ref, slot_ref, gate_ref, cnt_ref, *, tm):
    xn = _rms(x_ref[...], g_ref[...])
    hn_ref[...] = xn.astype(BF16)
    xh, xl = _split_bf16(xn)
    wh, wl = _split_bf16(wr_ref[...])
    logits = _dot(xh, wh) + _dot(xh, wl) + _dot(xl, wh)
    lane = lax.broadcasted_iota(jnp.int32, (tm, LANES), 1).astype(F32)
    logits = jnp.where(lane < N_EXPERTS, logits, NEG)
    m1 = jnp.max(logits, axis=1, keepdims=True)
    i1 = jnp.min(jnp.where(logits == m1, lane, float(LANES)), axis=1, keepdims=True)
    sel1 = lane == i1
    rest = jnp.where(sel1, NEG, logits)
    m2 = jnp.max(rest, axis=1, keepdims=True)
    i2 = jnp.min(jnp.where(rest == m2, lane, float(LANES)), axis=1, keepdims=True)
    sel2 = lane == i2
    e = jnp.exp(m2 - m1)
    gate_ref[...] = jnp.where(sel1, 1.0 / (1.0 + e), 0.0) + jnp.where(sel2, e / (1.0 + e), 0.0)
    sel = jnp.where(sel1 | sel2, 1.0, 0.0)
    sub = ATT_BLK
    r = lax.broadcasted_iota(jnp.int32, (sub, sub), 0)
    c = lax.broadcasted_iota(jnp.int32, (sub, sub), 1)
    before = jnp.where(r > c, 1.0, 0.0).astype(BF16)
    count = jnp.zeros((1, LANES), F32)
    for b in range(tm // sub):
        sel_b = sel[b * sub:(b + 1) * sub]
        rank = _dot(before, sel_b.astype(BF16)) + count
        slot_ref[b * sub:(b + 1) * sub, :] = jnp.where(sel_b > 0.5, rank, -1.0)
        count = count + jnp.sum(sel_b, axis=0, keepdims=True)
    cnt_ref[...] = jnp.broadcast_to(count, (SUBLANES, LANES))


def router(x, g, w_router, tm):
    tt, d = x.shape
    nt = tt // tm
    wr = jnp.zeros((d, LANES), F32).at[:, :N_EXPERTS].set(w_router)
    return pl.pallas_call(
        functools.partial(_router_kernel, tm=tm),
        grid=(nt,),
        in_specs=[pl.BlockSpec((tm, d), lambda i: (i, 0)),
                  pl.BlockSpec((1, d), lambda i: (0, 0)),
                  pl.BlockSpec((d, LANES), lambda i: (0, 0))],
        out_specs=[pl.BlockSpec((tm, d), lambda i: (i, 0)),
                   pl.BlockSpec((tm, LANES), lambda i: (i, 0)),
                   pl.BlockSpec((tm, LANES), lambda i: (i, 0)),
                   pl.BlockSpec((None, SUBLANES, LANES), lambda i: (i, 0, 0))],
        out_shape=[jax.ShapeDtypeStruct((tt, d), BF16),
                   jax.ShapeDtypeStruct((tt, LANES), F32),
                   jax.ShapeDtypeStruct((tt, LANES), F32),
                   jax.ShapeDtypeStruct((nt, SUBLANES, LANES), F32)],
        compiler_params=_cp(("parallel",)),
        name="router",
    )(x, g.reshape(1, d), wr)


def _moe_kernel(cnt_ref, hn_ref, slotc_ref, slotr_ref, gater_ref, wg_ref, wu_ref, wd_ref,
                out_ref, xc_ref, yacc_ref, *, tm, rows):
    i, e, f = pl.program_id(0), pl.program_id(1), pl.program_id(2)
    nf = pl.num_programs(2)
    nb = (cnt_ref[i * N_EXPERTS + e] + rows - 1) // rows
    slot_r = slotr_ref[...]

    def one_hot_rows(rb):
        rid = (lax.broadcasted_iota(jnp.int32, (rows, tm), 0) + rb * rows).astype(F32)
        return slot_r == rid

    @pl.when((e == 0) & (f == 0))
    def _():
        out_ref[...] = jnp.zeros_like(out_ref)

    @pl.when(f == 0)
    def _():
        def body(rb, carry):
            p = jnp.where(one_hot_rows(rb), 1.0, 0.0).astype(BF16)
            r0 = pl.multiple_of(rb * rows, rows)
            xc_ref[pl.ds(r0, rows), :] = _dot(p, hn_ref[...]).astype(BF16)
            return carry
        lax.fori_loop(0, nb, body, 0)

    def ffn_body(rb, carry):
        r0 = pl.multiple_of(rb * rows, rows)
        xs = xc_ref[pl.ds(r0, rows), :]
        a = (_silu(_dot(xs, wg_ref[...])) * _dot(xs, wu_ref[...])).astype(BF16)
        y = _dot(a, wd_ref[...])

        @pl.when(f == 0)
        def _():
            yacc_ref[pl.ds(r0, rows), :] = y

        @pl.when(f > 0)
        def _():
            yacc_ref[pl.ds(r0, rows), :] += y
        return carry
    lax.fori_loop(0, nb, ffn_body, 0)

    @pl.when(f == nf - 1)
    def _():
        lane = lax.broadcasted_iota(jnp.int32, (tm, LANES), 1)
        slot_c = jnp.sum(jnp.where(lane == e, slotc_ref[...], 0.0), axis=1, keepdims=True)
        gate_r = gater_ref[...]

        def body(rb, carry):
            r0 = pl.multiple_of(rb * rows, rows)
            gc = jnp.sum(jnp.where(one_hot_rows(rb), gate_r, 0.0), axis=1, keepdims=True)
            ys = (yacc_ref[pl.ds(r0, rows), :] * gc).astype(BF16)
            cid = (lax.broadcasted_iota(jnp.int32, (tm, rows), 1) + rb * rows).astype(F32)
            pt = jnp.where(slot_c == cid, 1.0, 0.0).astype(BF16)
            out_ref[...] += _dot(pt, ys)
            return carry
        lax.fori_loop(0, nb, body, 0)


def moe_ffn(x, g, w_router, wg, wu, wd, tm=MOE_TILE, tf=MOE_CHUNK, rows=MOE_ROWS):
    tt, d = x.shape
    nt = tt // tm
    dff = wg.shape[2]
    hn, slot, gate, cnt = router(x, g, w_router, tm)
    cnt_i = cnt[:, 0, :N_EXPERTS].astype(jnp.int32).reshape(nt * N_EXPERTS)

    def to_rows(a):
        a = a[:, :N_EXPERTS].reshape(nt, tm, N_EXPERTS)
        return a.transpose(0, 2, 1).reshape(nt, N_EXPERTS, 1, tm)

    grid_spec = pltpu.PrefetchScalarGridSpec(
        num_scalar_prefetch=1,
        grid=(nt, N_EXPERTS, dff // tf),
        in_specs=[pl.BlockSpec((tm, d), lambda i, e, f, c: (i, 0)),
                  pl.BlockSpec((tm, LANES), lambda i, e, f, c: (i, 0)),
                  pl.BlockSpec((None, None, 1, tm), lambda i, e, f, c: (i, e, 0, 0)),
                  pl.BlockSpec((None, None, 1, tm), lambda i, e, f, c: (i, e, 0, 0)),
                  pl.BlockSpec((None, d, tf), lambda i, e, f, c: (e, 0, f)),
                  pl.BlockSpec((None, d, tf), lambda i, e, f, c: (e, 0, f)),
                  pl.BlockSpec((None, tf, d), lambda i, e, f, c: (e, f, 0))],
        out_specs=pl.BlockSpec((tm, d), lambda i, e, f, c: (i, 0)),
        scratch_shapes=[pltpu.VMEM((_round_up(tm, rows), d), BF16), pltpu.VMEM((_round_up(tm, rows), d), F32)],
    )
    return pl.pallas_call(
        functools.partial(_moe_kernel, tm=tm, rows=rows),
        grid_spec=grid_spec,
        out_shape=jax.ShapeDtypeStruct((tt, d), F32),
        compiler_params=_cp(("parallel", "arbitrary", "arbitrary"), MOE_VMEM_LIMIT),
        name="moe_ffn",
    )(cnt_i, hn, slot, to_rows(slot), to_rows(gate), wg, wu, wd)


def _after_matrix(n):
    r = lax.broadcasted_iota(jnp.int32, (n, n), 0)
    c = lax.broadcasted_iota(jnp.int32, (n, n), 1)
    return jnp.where(r > c, 1.0, 0.0).astype(BF16)


def _sb_logs(z, valid):
    t = jnp.log(1.0 + jnp.exp2(jnp.abs(z) * -LOG2E))
    log_beta = jnp.minimum(z, 0.0) - t
    log_keep = log_beta - z
    if valid is not None:
        log_keep = jnp.where(valid, log_keep, 0.0)
    return log_beta, log_keep


def _sb_keep_after(log_keep, after):
    return _dot(jnp.concatenate(_split_bf16(log_keep), axis=1), jnp.concatenate([after, after], axis=0))


def _sb_weights(log_beta, log_keep, keep_after, carry, valid):
    w = jnp.exp(log_beta + keep_after + carry)
    if valid is not None:
        w = jnp.where(valid, w, 0.0)
    return w, carry + keep_after[:, 0:1] + log_keep[:, 0:1]


def _sb_block(z, valid, after, carry):
    log_beta, log_keep = _sb_logs(z, valid)
    return _sb_weights(log_beta, log_keep, _sb_keep_after(log_keep, after), carry, valid)


def _head_pair(q2):
    lane = lax.broadcasted_iota(jnp.int32, q2.shape, 1)
    zero = jnp.zeros_like(q2)
    return jnp.where(lane < DH, q2, zero), jnp.where(lane >= DH, q2, zero)


def _sb_prompt_kernel(o_hbm_ref, q_ref, k_ref, v_ref, o_ref, acc_ref, car_ref, *, blk, n_pair, tail):
    del o_hbm_ref
    qi = pl.program_id(2)
    n_head = 2 * n_pair
    after = _after_matrix(blk)

    def run(rows):
        qh = []
        for p in range(n_pair):
            qh += list(_head_pair(q_ref[0:rows, p * LANES:(p + 1) * LANES]))
        acc_ref[:, 0:rows, :] = jnp.zeros((n_head, rows, LANES), F32)
        car_ref[:, 0:rows, :] = jnp.zeros((n_head, rows, 1), F32)
        row = lax.broadcasted_iota(jnp.int32, (rows, blk), 0)
        col = lax.broadcasted_iota(jnp.int32, (rows, blk), 1)

        def block(kj, valid):
            k0 = pl.multiple_of(kj * blk, blk)
            kk = [k_ref[pl.ds(k0, blk), p * LANES:(p + 1) * LANES] for p in range(n_pair)]
            vv = [v_ref[pl.ds(k0, blk), p * LANES:(p + 1) * LANES] for p in range(n_pair)]
            zs = [_dot_nt(qh[h], kk[h // 2]) for h in range(n_head)]
            logs = [_sb_logs(z, valid) for z in zs]
            keep_after = [_sb_keep_after(lk, after) for _, lk in logs]
            ws = []
            for h in range(n_head):
                w, car_ref[h, 0:rows, :] = _sb_weights(logs[h][0], logs[h][1], keep_after[h],
                                                       car_ref[h, 0:rows, :], valid)
                ws.append(w.astype(BF16))
            for h in range(n_head):
                acc_ref[h, 0:rows, :] += _dot(ws[h], vv[h // 2])

        block(qi, col < row)

        def body(s, carry):
            block(qi - 1 - s, None)
            return carry
        lax.fori_loop(0, qi, body, 0)
        lane = lax.broadcasted_iota(jnp.int32, (rows, LANES), 1)
        for p in range(n_pair):
            o_ref[0:rows, p * LANES:(p + 1) * LANES] = jnp.where(lane < DH, acc_ref[2 * p, 0:rows, :],
                                                                 acc_ref[2 * p + 1, 0:rows, :])

    _run_with_short_last_block(run, qi, pl.num_programs(2) - 1, blk, tail, o_ref)


def _run_with_short_last_block(run, qi, last, blk, tail, o_ref):
    if tail == blk:
        run(blk)
        return

    @pl.when(qi < last)
    def _():
        run(blk)

    @pl.when(qi == last)
    def _():
        o_ref[...] = jnp.zeros_like(o_ref)
        run(tail)


def _last_block_rows(n_valid, lpad, blk):
    tail = n_valid - (lpad - blk)
    assert 0 < tail <= blk and tail % (2 * SUBLANES) == 0
    return tail


def sb_prompt_attention(o_init, q_bf, k_bf, v_bf, n_batch, lpad, n_valid, blk=ATT_BLK, n_pair=ATT_PAIRS):
    tt, d = q_bf.shape
    nq = lpad // blk
    tail = _last_block_rows(n_valid, lpad, blk)
    wid = n_pair * LANES
    return pl.pallas_call(
        functools.partial(_sb_prompt_kernel, blk=blk, n_pair=n_pair, tail=tail),
        grid=(n_batch, d // wid, nq),
        in_specs=[pl.BlockSpec(memory_space=pl.ANY),
                  pl.BlockSpec((blk, wid), lambda b, h, i: (b * nq + i, h)),
                  pl.BlockSpec((lpad, wid), lambda b, h, i: (b, h)),
                  pl.BlockSpec((lpad, wid), lambda b, h, i: (b, h))],
        out_specs=pl.BlockSpec((blk, wid), lambda b, h, i: (b * nq + i, h)),
        out_shape=jax.ShapeDtypeStruct((tt, d), F32),
        scratch_shapes=[pltpu.VMEM((2 * n_pair, blk, LANES), F32), pltpu.VMEM((2 * n_pair, blk, 1), F32)],
        input_output_aliases={0: 0},
        compiler_params=_cp(("parallel", "parallel", "arbitrary")),
        name="sb_prompt_attention",
    )(o_init, q_bf, k_bf, v_bf)


def _lambda(lam_ref, lam_init):
    a = jnp.sum(lam_ref[0:1, :] * lam_ref[1:2, :], axis=1, keepdims=True)
    b = jnp.sum(lam_ref[2:3, :] * lam_ref[3:4, :], axis=1, keepdims=True)
    return jnp.exp(a) - jnp.exp(b) + lam_init


def _diff_prompt_kernel(o_hbm_ref, lam_ref, g_ref, q_ref, k_ref, v_ref, o_ref, m_ref, l_ref, acc_ref,
                        *, blk, n_pair, tail, lam_init):
    del o_hbm_ref
    qi = pl.program_id(2)
    n_map = 2 * n_pair
    ones = jnp.ones((blk, LANES), BF16)

    def run(rows):
        qh = []
        for p in range(n_pair):
            qh += list(_head_pair(q_ref[0:rows, p * LANES:(p + 1) * LANES]))
        m_ref[:, 0:rows, :] = jnp.full((n_map, rows, LANES), NEG, F32)
        l_ref[:, 0:rows, :] = jnp.zeros((n_map, rows, LANES), F32)
        acc_ref[:, 0:rows, :] = jnp.zeros((n_map, rows, LANES), F32)
        row = lax.broadcasted_iota(jnp.int32, (rows, blk), 0)
        col = lax.broadcasted_iota(jnp.int32, (rows, blk), 1)

        def block(kj, valid):
            k0 = pl.multiple_of(kj * blk, blk)
            kk = [k_ref[pl.ds(k0, blk), p * LANES:(p + 1) * LANES] for p in range(n_pair)]
            vv = [v_ref[pl.ds(k0, blk), p * LANES:(p + 1) * LANES] for p in range(n_pair)]
            ss = [_dot_nt(qh[h], kk[h // 2]) for h in range(n_map)]
            if valid is not None:
                ss = [jnp.where(valid, s, NEG) for s in ss]
            ps, alphas = [], []
            for h in range(n_map):
                m_old = m_ref[h, 0:rows, :]
                m_new = jnp.maximum(m_old, jnp.max(ss[h], axis=1, keepdims=True))
                alphas.append(jnp.exp(m_old - m_new))
                ps.append(jnp.exp(ss[h] - jnp.concatenate([m_new] * (blk // LANES), axis=1)).astype(BF16))
                m_ref[h, 0:rows, :] = m_new
            v1 = [jnp.concatenate([v, ones], axis=1) for v in vv]
            for h in range(n_map):
                pv = _dot(ps[h], v1[h // 2])
                acc_ref[h, 0:rows, :] = alphas[h] * acc_ref[h, 0:rows, :] + pv[:, :LANES]
                l_ref[h, 0:rows, :] = alphas[h] * l_ref[h, 0:rows, :] + pv[:, LANES:]

        block(qi, col <= row)

        def body(s, carry):
            block(s, None)
            return carry
        lax.fori_loop(0, qi, body, 0)
        lam = _lambda(lam_ref, lam_init)
        for p in range(n_pair):
            o = (acc_ref[2 * p, 0:rows, :] / l_ref[2 * p, 0:rows, :]
                 - lam * (acc_ref[2 * p + 1, 0:rows, :] / l_ref[2 * p + 1, 0:rows, :]))
            o_ref[0:rows, p * LANES:(p + 1) * LANES] = _rms(o, g_ref[...]) * (1.0 - lam_init)

    _run_with_short_last_block(run, qi, pl.num_programs(2) - 1, blk, tail, o_ref)


def diff_prompt_attention(o_init, lam4, subln, q_bf, k_bf, v_bf, n_batch, lpad, n_valid, lam_init,
                          blk=ATT_BLK, n_pair=ATT_PAIRS):
    tt, d = q_bf.shape
    nq = lpad // blk
    tail = _last_block_rows(n_valid, lpad, blk)
    wid = n_pair * LANES
    return pl.pallas_call(
        functools.partial(_diff_prompt_kernel, blk=blk, n_pair=n_pair, tail=tail, lam_init=lam_init),
        grid=(n_batch, d // wid, nq),
        in_specs=[pl.BlockSpec(memory_space=pl.ANY),
                  pl.BlockSpec((4, DH), lambda b, h, i: (0, 0)),
                  pl.BlockSpec((1, LANES), lambda b, h, i: (0, 0)),
                  pl.BlockSpec((blk, wid), lambda b, h, i: (b * nq + i, h)),
                  pl.BlockSpec((lpad, wid), lambda b, h, i: (b, h)),
                  pl.BlockSpec((lpad, wid), lambda b, h, i: (b, h))],
        out_specs=pl.BlockSpec((blk, wid), lambda b, h, i: (b * nq + i, h)),
        out_shape=jax.ShapeDtypeStruct((tt, d), F32),
        scratch_shapes=[pltpu.VMEM((2 * n_pair, blk, LANES), F32)] * 3,
        input_output_aliases={0: 0},
        compiler_params=_cp(("parallel", "parallel", "arbitrary")),
        name="diff_prompt_attention",
    )(o_init, lam4, subln.reshape(1, LANES), q_bf, k_bf, v_bf)


def _block_diag_mask(n_heads_rows, lane_group):
    shape = (n_heads_rows * SUBLANES, D_MODEL)
    rh = lax.broadcasted_iota(jnp.int32, shape, 0) // SUBLANES
    lh = lax.broadcasted_iota(jnp.int32, shape, 1) // lane_group
    return rh, lh


def _page_tokens(ref):
    n_heads = ref.shape[0] // LANES
    heads = [ref[pl.ds(h, LANES, stride=n_heads), :] for h in range(n_heads)]
    return jnp.concatenate(heads, axis=1).astype(BF16)


def _page_values(w, v_ref, v_by_width, precise=False):
    if v_by_width:
        return _mm(w, v_ref[...], precise, nt=True)
    assert not precise
    return _dot(w.astype(BF16), _page_tokens(v_ref))


def _new_token_masks():
    key = lax.broadcasted_iota(jnp.int32, (LANES, LANES), 1)
    tq = lax.broadcasted_iota(jnp.int32, (LANES, LANES), 0) % SUBLANES
    return key, tq


def _sample_setup(q_ref, kn_ref, vn_ref, qbd_ref, kpad_ref, vpad_ref):
    q = q_ref[...] * (DH ** -0.5)
    qt = jnp.broadcast_to(q[None], (H_SB, SUBLANES, D_MODEL)).reshape(H_SB * SUBLANES, D_MODEL)
    rh, lh = _block_diag_mask(H_SB, DH)
    qbd_ref[...] = jnp.where(rh == lh, qt, 0.0).astype(qbd_ref.dtype)
    kpad_ref[...] = jnp.zeros_like(kpad_ref)
    vpad_ref[...] = jnp.zeros_like(vpad_ref)
    kpad_ref[0:SUBLANES, :] = kn_ref[...]
    vpad_ref[0:SUBLANES, :] = vn_ref[...]


def _sb_sample_kernel(pt_ref, o_hbm_ref, q_ref, kn_ref, vn_ref, *rest, n_pg, v_by_width, precise):
    del pt_ref, o_hbm_ref
    k_refs, v_refs = rest[:n_pg], rest[n_pg:2 * n_pg]
    o_ref, qbd_ref, acc_ref, car_ref, kpad_ref, vpad_ref = rest[2 * n_pg:]
    st = pl.program_id(1)
    after = _after_matrix(LANES)

    @pl.when(st == 0)
    def _():
        _sample_setup(q_ref, kn_ref, vn_ref, qbd_ref, kpad_ref, vpad_ref)
        key, tq = _new_token_masks()
        z = _mm(qbd_ref[...], kpad_ref[...], precise, nt=True)
        w, car_ref[...] = _sb_block(z, key < tq, after, jnp.zeros(car_ref.shape, F32))
        acc_ref[...] = _mm(w, vpad_ref[...], precise)

    @pl.when(st > 0)
    def _():
        order = list(reversed(range(n_pg)))
        qbd = qbd_ref[...]
        zs = [_mm(qbd, k_refs[p][...], precise) for p in order]
        logs = [_sb_logs(z, None) for z in zs]
        keep_after = [_sb_keep_after(lk, after) for _, lk in logs]
        car = car_ref[...]
        ws = []
        for i in range(n_pg):
            w, car = _sb_weights(logs[i][0], logs[i][1], keep_after[i], car, None)
            ws.append(w)
        car_ref[...] = car
        acc = acc_ref[...]
        for i, p in enumerate(order):
            acc = acc + _page_values(ws[i], v_refs[p], v_by_width, precise)
        acc_ref[...] = acc

    @pl.when(st == pl.num_programs(1) - 1)
    def _():
        rh, lh = _block_diag_mask(H_SB, DH)
        a = jnp.where(rh == lh, acc_ref[...], 0.0).reshape(H_SB, SUBLANES, D_MODEL)
        o_ref[...] = jnp.sum(a, axis=0)


def _diff_sample_kernel(pt_ref, o_hbm_ref, lam_ref, g_ref, q_ref, kn_ref, vn_ref, *rest, n_pg, v_by_width,
                        lam_init):
    del pt_ref, o_hbm_ref
    k_refs, v_refs = rest[:n_pg], rest[n_pg:2 * n_pg]
    o_ref, qbd_ref, acc_ref, m_ref, l_ref, kpad_ref, vpad_ref = rest[2 * n_pg:]
    st = pl.program_id(1)

    @pl.when(st == 0)
    def _():
        _sample_setup(q_ref, kn_ref, vn_ref, qbd_ref, kpad_ref, vpad_ref)
        key, tq = _new_token_masks()
        s = jnp.where(key <= tq, _dot_nt(qbd_ref[...], kpad_ref[...].astype(BF16)), NEG)
        m = jnp.max(s, axis=1, keepdims=True)
        p = jnp.exp(s - m)
        m_ref[...] = m
        l_ref[...] = jnp.sum(p, axis=1, keepdims=True)
        acc_ref[...] = _dot(p.astype(BF16), vpad_ref[...].astype(BF16))

    @pl.when(st > 0)
    def _():
        qbd = qbd_ref[...]
        ss = [_dot(qbd, k_refs[p][...].astype(BF16)) for p in range(n_pg)]
        top = ss[0]
        for s in ss[1:]:
            top = jnp.maximum(top, s)
        m_old = m_ref[...]
        m_new = jnp.maximum(m_old, jnp.max(top, axis=1, keepdims=True))
        alpha = jnp.exp(m_old - m_new)
        ps = [jnp.exp(s - m_new) for s in ss]
        tot = ps[0]
        for p in ps[1:]:
            tot = tot + p
        l_ref[...] = alpha * l_ref[...] + jnp.sum(tot, axis=1, keepdims=True)
        m_ref[...] = m_new
        acc = alpha * acc_ref[...]
        for i in range(n_pg):
            acc = acc + _page_values(ps[i].astype(BF16), v_refs[i], v_by_width)
        acc_ref[...] = acc

    @pl.when(st == pl.num_programs(1) - 1)
    def _():
        lam = _lambda(lam_ref, lam_init)
        rh, lh = _block_diag_mask(2 * H_DIFF, 2 * DH)
        coef = jnp.where(rh % 2 == 0, 1.0, -lam)
        a = jnp.where(rh // 2 == lh, acc_ref[...] / l_ref[...] * coef, 0.0)
        o = jnp.sum(a.reshape(2 * H_DIFF, SUBLANES, D_MODEL), axis=0)
        for h in range(H_DIFF):
            sl = slice(h * 2 * DH, (h + 1) * 2 * DH)
            o_ref[:, sl] = _rms(o[:, sl], g_ref[...]) * (1.0 - lam_init)


def sample_attention(kind, o_init, page_table, q, k_new, v_new, cache_k, cache_v, layer, row0, n_seq,
                     lam4=None, subln=None, lam_init=None, n_pg=PAGES_PER_STEP, q_row0=None, precise=False):
    tt, d = o_init.shape
    n_pages = page_table.shape[1]
    page = cache_k.shape[2]
    assert page == LANES and n_pages % n_pg == 0 and row0 % SUBLANES == 0
    n_grp = n_pages // n_pg
    n_pool = cache_k.shape[1]
    def page_view(c):
        n_layers, _, _, heads, width = c.shape
        if width < LANES:
            return jnp.transpose(c, (0, 1, 3, 4, 2)).reshape(n_layers, n_pool, heads * width, page), True
        return c.reshape(n_layers, n_pool, page * heads, width), False

    ck, k_by_width = page_view(cache_k)
    cv, v_by_width = page_view(cache_v)
    assert k_by_width and ck.shape[2:] == cv.shape[2:] == (d, LANES)
    blk0 = row0 // SUBLANES
    blk0_q = blk0 if q_row0 is None else q_row0 // SUBLANES

    def tok_spec(first=None):
        first = blk0_q if first is None else first
        return pl.BlockSpec((SUBLANES, d), lambda s, st, pt: (first + s, 0))

    def page_spec(p):
        if kind == "sb":
            def im(s, st, pt):
                grp = jnp.minimum(n_grp - st, n_grp - 1)
                return (layer, pt[s, grp * n_pg + p], 0, 0)
        else:
            def im(s, st, pt):
                grp = jnp.maximum(st - 1, 0)
                return (layer, pt[s, grp * n_pg + p], 0, 0)
        return pl.BlockSpec((None, None, d, LANES), im)

    rows = H_SB * SUBLANES
    common_scratch = [pltpu.VMEM((rows, d), F32 if precise else BF16), pltpu.VMEM((rows, d), F32)]
    pad_scratch = [pltpu.VMEM((LANES, d), F32), pltpu.VMEM((LANES, d), F32)]
    page_specs = [page_spec(p) for p in range(n_pg)] * 2
    page_args = [ck] * n_pg + [cv] * n_pg
    if kind == "sb":
        body = functools.partial(_sb_sample_kernel, n_pg=n_pg, v_by_width=v_by_width, precise=precise)
        extra_specs, extra_args = [], []
        scratch = common_scratch + [pltpu.VMEM((rows, 1), F32)] + pad_scratch
    else:
        body = functools.partial(_diff_sample_kernel, n_pg=n_pg, v_by_width=v_by_width, lam_init=lam_init)
        extra_specs = [pl.BlockSpec((4, DH), lambda s, st, pt: (0, 0)),
                       pl.BlockSpec((1, LANES), lambda s, st, pt: (0, 0))]
        extra_args = [lam4, subln.reshape(1, LANES)]
        scratch = common_scratch + [pltpu.VMEM((rows, 1), F32), pltpu.VMEM((rows, 1), F32)] + pad_scratch
    grid_spec = pltpu.PrefetchScalarGridSpec(
        num_scalar_prefetch=1,
        grid=(n_seq, n_grp + 1),
        in_specs=[pl.BlockSpec(memory_space=pl.ANY)] + extra_specs
                 + [tok_spec(), tok_spec(), tok_spec()] + page_specs,
        out_specs=tok_spec(blk0),
        scratch_shapes=scratch,
    )
    return pl.pallas_call(
        body,
        grid_spec=grid_spec,
        out_shape=jax.ShapeDtypeStruct((tt, d), F32),
        input_output_aliases={1: 0},
        compiler_params=_cp(("parallel", "arbitrary")),
        name=kind + "_sample_attention",
    )(page_table, o_init, *extra_args, q, k_new, v_new, *page_args)


def _conv_kernel(gb_ref, gc_ref, hv_ref, hgc_ref, hhv_ref, p1_ref, p2_ref, w_ref, o_ref, u_ref,
                 *, tiles_per_seq, sample_tile, dec_seq):
    i = pl.program_id(0)
    u = gc_ref[...] * hv_ref[...]
    u_ref[...] = u
    row = lax.broadcasted_iota(jnp.int32, u.shape, 0)
    r1 = pltpu.roll(u, 1, axis=0)
    r2 = pltpu.roll(u, 2, axis=0)
    w0, w1, w2 = w_ref[0:1, :], w_ref[1:2, :], w_ref[2:3, :]

    def finish(prev1, prev2):
        o_ref[...] = gb_ref[...] * (w0 * prev2 + w1 * prev1 + w2 * u)

    @pl.when(i == sample_tile)
    def _():
        pos = row % dec_seq
        finish(jnp.where(pos == 0, p1_ref[...], r1), jnp.where(pos < 2, p2_ref[...], r2))

    @pl.when(i != sample_tile)
    def _():
        hu = hgc_ref[...] * hhv_ref[...]
        hu = jnp.where(i % tiles_per_seq == 0, 0.0, hu)
        h7, h6 = hu[7:8, :], hu[6:7, :]
        finish(jnp.where(row == 0, h7, r1), jnp.where(row == 0, h6, jnp.where(row == 1, h7, r2)))


def gated_conv(gb, gc, hv, prev1, prev2, w, lpad, sample_row0, dec_seq, tc=ATT_BLK):
    tt, d = gb.shape
    assert lpad % tc == 0 and sample_row0 % tc == 0 and prev1.shape[0] == tc
    per = tc // SUBLANES

    def tile():
        return pl.BlockSpec((tc, d), lambda i: (i, 0))

    def halo():
        return pl.BlockSpec((SUBLANES, d), lambda i: (jnp.maximum(i * per - 1, 0), 0))

    def whole(r):
        return pl.BlockSpec((r, d), lambda i: (0, 0))

    return pl.pallas_call(
        functools.partial(_conv_kernel, tiles_per_seq=lpad // tc, sample_tile=sample_row0 // tc,
                          dec_seq=dec_seq),
        grid=(tt // tc,),
        in_specs=[tile(), tile(), tile(), halo(), halo(), whole(tc), whole(tc), whole(w.shape[0])],
        out_specs=[tile(), tile()],
        out_shape=[jax.ShapeDtypeStruct((tt, d), F32), jax.ShapeDtypeStruct((tt, d), F32)],
        compiler_params=_cp(("parallel",)),
        name="gated_conv",
    )(gb, gc, hv, gc, hv, prev1, prev2, w)


def _norm_kernel(x_ref, d_ref, g_ref, o_ref, *, has_delta):
    x = x_ref[...]
    if has_delta:
        x = x + d_ref[...]
    o_ref[...] = _rms(x, g_ref[...])


def final_norm(x, g, delta=None, tm=ROW_TILE):
    tt, d = x.shape
    d_arg, d_spec = _delta_operand(x, delta, tm)
    return pl.pallas_call(
        functools.partial(_norm_kernel, has_delta=delta is not None),
        grid=(tt // tm,),
        in_specs=[pl.BlockSpec((tm, d), lambda i: (i, 0)), d_spec, pl.BlockSpec((1, d), lambda i: (0, 0))],
        out_specs=pl.BlockSpec((tm, d), lambda i: (i, 0)),
        out_shape=jax.ShapeDtypeStruct((tt, d), F32),
        compiler_params=_cp(("parallel",)),
        name="final_norm",
    )(x, d_arg, g.reshape(1, d))


def _round_up(a, b):
    return -(-a // b) * b


def kernel(x_prompt, x_sample, cache_k_sb, cache_v_sb, cache_k_diff, cache_v_diff, state_conv, page_table,
           meta_tokens, norm_mix, w_mix_in, w_mix_out, conv_w, diff_lambda_q1, diff_lambda_k1,
           diff_lambda_q2, diff_lambda_k2, diff_subln, norm_ffn, w_dense_gate, w_dense_up, w_dense_down,
           w_router, w_exp_gate, w_exp_up, w_exp_down, norm_final):
    bp, seq, d = x_prompt.shape
    bs, ts, _ = x_sample.shape
    depth = w_mix_in.shape[0]
    assert d == D_MODEL and ts == SUBLANES and H_SB * ts == LANES
    lp = seq + N_META
    lpad = _round_up(lp, ATT_BLK)
    n_s = bs * ts
    row_s = bp * lpad
    tt = _round_up(row_s + n_s, math.lcm(ROW_TILE, FFN_TILE, MOE_TILE, ATT_BLK))
    assert n_s == ATT_BLK

    meta = jnp.broadcast_to(meta_tokens[None].astype(F32), (bp, N_META, d))
    xp = jnp.concatenate([meta, x_prompt, jnp.zeros((bp, lpad - lp, d), F32)], axis=1)
    x = jnp.concatenate([xp.reshape(row_s, d), x_sample.reshape(n_s, d),
                         jnp.zeros((tt - row_s - n_s, d), F32)], axis=0)

    def prompt_rows(a):
        return a[:row_s].reshape(bp, lpad, -1)[:, :lp]

    def sample_rows(a):
        return a[row_s:row_s + n_s].reshape(bs, ts, -1)

    outs = {name: [] for name in ("sb_kp", "sb_vp", "sb_ks", "sb_vs", "df_kp", "df_vp", "df_ks", "df_vs",
                                  "cv_p", "cv_s")}
    att_plan = [(True, DH ** -0.5), (True, 1.0), (True, 1.0)]
    pending = None
    xs = x_sample.reshape(n_s, d).astype(F32)
    for i in range(depth):
        kind, j = i % N_MIXERS, i // N_MIXERS
        w_in = w_mix_in[i].astype(BF16)
        if kind == 1:
            proj = norm_matmul(x, norm_mix[i], w_in, [(True, None)] * 3, delta=pending)
            if pending is not None:
                x, proj = proj[0], proj[1:]
            gb, gc, hv = proj
            st = state_conv[j].astype(F32)
            zero = jnp.zeros((bs, ts, d), F32)
            prev1 = zero.at[:, 0].set(st[:, 1]).reshape(n_s, d)
            prev2 = zero.at[:, 0].set(st[:, 0]).at[:, 1].set(st[:, 1]).reshape(n_s, d)
            o, u = gated_conv(gb, gc, hv, prev1, prev2, conv_w[j], lpad, row_s, ts)
            if xs is not None:
                proj_s = norm_matmul(xs, norm_mix[i], w_mix_in[i], [(True, None)] * 3, precise=True)
                o_s, _ = gated_conv(*proj_s, prev1, prev2, conv_w[j], ATT_BLK, 0, ts)
            outs["cv_p"].append(prompt_rows(u)[:, -(conv_w.shape[1] - 1):])
            outs["cv_s"].append(sample_rows(u)[:, -(conv_w.shape[1] - 1):])
        else:
            proj = norm_matmul(x, norm_mix[i], w_in, att_plan, delta=pending)
            if pending is not None:
                x, proj = proj[0], proj[1:]
            q, q_bf, k, k_bf, v, v_bf = proj
            o = jnp.zeros((tt, d), F32)
            if kind == 0:
                o = sb_prompt_attention(o, q_bf, k_bf, v_bf, bp, lpad, lp)
                if xs is not None:
                    q_s, k_s, v_s = norm_matmul(xs, norm_mix[i], w_mix_in[i], [(True, None)] * 3, precise=True)
                    o = sample_attention("sb", o, page_table, q_s, k_s, v_s, cache_k_sb, cache_v_sb, j, row_s,
                                         bs, q_row0=0, precise=True)
                    o_s = o[row_s:row_s + n_s]
                else:
                    o = sample_attention("sb", o, page_table, q, k, v, cache_k_sb, cache_v_sb, j, row_s, bs)
                pre, hk, hv_ = "sb", (H_SB, DH), (H_SB, DH)
            else:
                xs = None
                lam_init = 0.8 - 0.6 * math.exp(-0.3 * i)
                lam4 = jnp.stack([diff_lambda_q1[j], diff_lambda_k1[j], diff_lambda_q2[j],
                                  diff_lambda_k2[j]]).astype(F32)
                o = diff_prompt_attention(o, lam4, diff_subln[j], q_bf, k_bf, v_bf, bp, lpad, lp, lam_init)
                o = sample_attention("diff", o, page_table, q, k, v, cache_k_diff, cache_v_diff, j, row_s,
                                     bs, lam4=lam4, subln=diff_subln[j], lam_init=lam_init)
                pre, hk, hv_ = "df", (2 * H_DIFF, DH), (H_DIFF, 2 * DH)
            outs[pre + "_kp"].append(prompt_rows(k).reshape(bp, lp, *hk))
            outs[pre + "_vp"].append(prompt_rows(v).reshape(bp, lp, *hv_))
            outs[pre + "_ks"].append(sample_rows(k).reshape(bs, ts, *hk))
            outs[pre + "_vs"].append(sample_rows(v).reshape(bs, ts, *hv_))
        x = matmul_residual(o, w_mix_out[i].astype(BF16), x)
        if xs is not None:
            xs = matmul_residual(o_s, w_mix_out[i], xs, precise=True)

        m = i // 2
        if i % 2 == 0:
            x = dense_ffn(x, norm_ffn[i], w_dense_gate[m].astype(BF16), w_dense_up[m].astype(BF16),
                          w_dense_down[m].astype(BF16))
            if xs is not None:
                xs = dense_ffn(xs, norm_ffn[i], w_dense_gate[m], w_dense_up[m], w_dense_down[m], precise=True)
            pending = None
        else:
            if xs is not None:
                x = lax.dynamic_update_slice(x, xs, (row_s, 0))
                xs = None
            pending = moe_ffn(x, norm_ffn[i], w_router[m], w_exp_gate[m].astype(BF16),
                              w_exp_up[m].astype(BF16), w_exp_down[m].astype(BF16))

    y = final_norm(x, norm_final, delta=pending)
    y_prompt = prompt_rows(y)[:, N_META:]
    y_sample = sample_rows(y)
    return (y_prompt, y_sample,
            jnp.stack(outs["sb_kp"]), jnp.stack(outs["sb_vp"]), jnp.stack(outs["sb_ks"]),
            jnp.stack(outs["sb_vs"]), jnp.stack(outs["df_kp"]), jnp.stack(outs["df_vp"]),
            jnp.stack(outs["df_ks"]), jnp.stack(outs["df_vs"]),
            jnp.stack(outs["cv_p"]), jnp.stack(outs["cv_s"]))
```
